```python
import math
import jax
import jax.numpy as jnp
from jax import lax
import numpy as np

D_MODEL = 1024
BATCH = 8
SEQ = 2048
DEPTH = 2
DEC_BATCH = 128
DEC_SEQ = 8
PAST_LEN = 2048
PAGE_SIZE = 128

PLE_DIM = 256
BRANCH_WIDTH = D_MODEL
A_WIDTH = BRANCH_WIDTH
A_GROUPS = 4
A_GROUP_WIDTH = A_WIDTH // A_GROUPS
CHUNK = 128
N_HEADS = 16
HEAD_DIM = BRANCH_WIDTH // N_HEADS
N_KV_HEADS = 4
HEADS_PER_KV = N_HEADS // N_KV_HEADS
CMP_LEN = 32
CMP_STRIDE = 16
CMP_HIDDEN = 256
SEL_LEN = 64
SEL_TOP = 16
SEL_QBLOCK = 64
WINDOW = 512
WIN_QBLOCK = 128
N_BUCKETS = 32
MAX_DISTANCE = 128
N_GROUPS = 4
EXPERTS_PER_GROUP = 8
N_EXPERTS = N_GROUPS * EXPERTS_PER_GROUP
TOP_K = 2
D_EXPERT = 512
MOE_BLOCK = 128
N_KV_SLOTS = 6
N_CACHE_SLOTS = 4
KV_COLS = N_KV_SLOTS * N_KV_HEADS * HEAD_DIM
NSA_GATE_COLS = 3 * N_HEADS
MERGE_GATE_COLS = 2 * D_MODEL
SPLIT_POINTS = (A_WIDTH, 2 * A_WIDTH, 2 * A_WIDTH + BRANCH_WIDTH,
                2 * A_WIDTH + BRANCH_WIDTH + KV_COLS,
                2 * A_WIDTH + BRANCH_WIDTH + KV_COLS + NSA_GATE_COLS)
IN_COLS = SPLIT_POINTS[-1] + MERGE_GATE_COLS
EPS = 1e-6
NEG = -1e30
FORCE = 1e9

kernel_name = 'hybrid_gmlp_nsa_hmoe_step'


def rms_norm(x, gain):
    xf = x.astype(jnp.float32)
    xf = xf * lax.rsqrt(jnp.mean(xf * xf, axis=-1, keepdims=True) + EPS)
    return (xf * gain.astype(jnp.float32)).astype(x.dtype)


def rel_bucket(dist):
    n = jnp.maximum(jnp.asarray(dist, jnp.int32), 0)
    max_exact = N_BUCKETS // 2
    nf = jnp.maximum(n, max_exact).astype(jnp.float32)
    large = max_exact + (jnp.log(nf / max_exact) / math.log(MAX_DISTANCE / max_exact)
                         * (N_BUCKETS - max_exact)).astype(jnp.int32)
    return jnp.where(n < max_exact, n, jnp.minimum(large, N_BUCKETS - 1))


def chunk_mlp(u, v, w_s, b_s):
    b, s = u.shape[:2]
    length = min(s, CHUNK)
    n_chunks = s // length
    causal = np.tril(np.ones((length, length), np.float32))
    w = w_s[:, :length, :length] * causal
    vg = v.reshape(b, n_chunks, length, A_GROUPS, A_GROUP_WIDTH)
    mixed = jnp.einsum('grs,bcsgw->bcrgw', w, vg) + b_s[:, :length].T[None, None, :, :, None]
    return u * mixed.reshape(b, s, A_WIDTH)


def compress(rows, pe, w1, w2):
    b, t = rows.shape[:2]
    n_cmp = (t - CMP_LEN) // CMP_STRIDE + 1
    idx = np.arange(n_cmp)[:, None] * CMP_STRIDE + np.arange(CMP_LEN)[None]
    blk = rows[:, idx] + pe[None, None, :, None, :]
    flat = blk.transpose(0, 1, 3, 2, 4).reshape(b, n_cmp, N_KV_HEADS, CMP_LEN * HEAD_DIM)
    return jax.nn.gelu(flat @ w1) @ w2


def cmp_sel_attn(q, hist, q_pos, rel_bias, pe_cmp, w_phi1, w_phi2):
    b, sq = q.shape[:2]
    t = hist.shape[1]
    scale = HEAD_DIM ** -0.5
    k_c = compress(hist[:, :, 0], pe_cmp[0], w_phi1[0], w_phi2[0])
    v_c = compress(hist[:, :, 1], pe_cmp[1], w_phi1[1], w_phi2[1])
    n_cmp = k_c.shape[1]
    c_start = np.arange(n_cmp) * CMP_STRIDE
    dist = q_pos[:, None] - (c_start + CMP_LEN - 1)[None]
    valid = dist >= 0
    bias = rel_bias[rel_bucket(dist)].reshape(sq, n_cmp, N_KV_HEADS, HEADS_PER_KV).transpose(2, 3, 0, 1)
    s = jnp.einsum('bqghd,bcgd->bghqc', q, k_c).astype(jnp.float32) * scale + bias
    p_cmp = jax.nn.softmax(jnp.where(valid, s, NEG), axis=-1) * valid
    o_cmp = jnp.einsum('bghqc,bcgd->bqghd', p_cmp.astype(v_c.dtype), v_c)
    n_sel = -(-t // SEL_LEN)
    s_start = np.arange(n_sel) * SEL_LEN
    overlap = ((c_start[:, None] < s_start[None] + SEL_LEN)
               & (c_start[:, None] + CMP_LEN > s_start[None])).astype(np.float32)
    importance = jnp.einsum('bghqc,cj->bgqj', p_cmp, overlap)
    q_blk = q_pos // SEL_LEN
    j = np.arange(n_sel)[None]
    forced = (j == 0) | (j == q_blk[:, None]) | (j == q_blk[:, None] - 1)
    causal_blk = j <= q_blk[:, None]
    score = jnp.where(forced, FORCE, jnp.where(causal_blk, importance, NEG))
    n_top = min(SEL_TOP, n_sel)
    sel = lax.top_k(score, n_top)[1].transpose(0, 2, 1, 3)
    pad = n_sel * SEL_LEN - t

    def to_blocks(r):
        r = jnp.pad(r, ((0, 0), (0, pad), (0, 0), (0, 0)))
        return r.reshape(b, n_sel, SEL_LEN, N_KV_HEADS, HEAD_DIM).transpose(0, 3, 1, 2, 4)

    k_s = to_blocks(hist[:, :, 2])
    v_s = to_blocks(hist[:, :, 3])
    rb_g = rel_bias.reshape(N_BUCKETS, N_KV_HEADS, HEADS_PER_KV).transpose(1, 0, 2)
    bi = jnp.arange(b)[:, None, None, None]
    gi = jnp.arange(N_KV_HEADS)[None, None, :, None]
    offs = jnp.arange(SEL_LEN)
    n_keys = n_top * SEL_LEN

    def select_block(args):
        q_b, sel_b, pos_b = args
        qb = q_b.shape[1]
        k_g = k_s[bi, gi, sel_b].reshape(b, qb, N_KV_HEADS, n_keys, HEAD_DIM)
        v_g = v_s[bi, gi, sel_b].reshape(b, qb, N_KV_HEADS, n_keys, HEAD_DIM)
        k_pos = (sel_b[..., None] * SEL_LEN + offs).reshape(b, qb, N_KV_HEADS, n_keys)
        d = pos_b[None, :, None, None] - k_pos
        bias_b = rb_g[gi, rel_bucket(d)]
        s_b = jnp.einsum('bqghd,bqgkd->bqghk', q_b, k_g).astype(jnp.float32) * scale
        s_b = s_b + jnp.swapaxes(bias_b, -1, -2)
        p_b = jax.nn.softmax(jnp.where((d >= 0)[:, :, :, None, :], s_b, NEG), axis=-1)
        return jnp.einsum('bqghk,bqgkd->bqghd', p_b.astype(v_g.dtype), v_g)

    qb = math.gcd(sq, SEL_QBLOCK)
    nqb = sq // qb
    o = lax.map(select_block, (
        jnp.moveaxis(q.reshape(b, nqb, qb, N_KV_HEADS, HEADS_PER_KV, HEAD_DIM), 1, 0),
        jnp.moveaxis(sel.reshape(b, nqb, qb, N_KV_HEADS, n_top), 1, 0),
        jnp.asarray(q_pos.reshape(nqb, qb), jnp.int32)))
    o_slc = jnp.moveaxis(o, 0, 1).reshape(b, sq, N_KV_HEADS, HEADS_PER_KV, HEAD_DIM)
    return o_cmp, o_slc


def window_attn(q, k, v, q_pos, k_pos, rel_bias):
    dist = q_pos[:, :, None] - k_pos[:, None, :]
    valid = (dist >= 0) & (dist < WINDOW) & (k_pos[:, None, :] >= 0)
    nb, qb, kw = dist.shape
    bias = rel_bias[rel_bucket(dist)].reshape(nb, qb, kw, N_KV_HEADS, HEADS_PER_KV).transpose(0, 3, 4, 1, 2)
    s = jnp.einsum('bnqghd,bnkgd->bnghqk', q, k).astype(jnp.float32) * HEAD_DIM ** -0.5 + bias
    p = jax.nn.softmax(jnp.where(valid[:, None, None], s, NEG), axis=-1)
    return jnp.einsum('bnghqk,bnkgd->bnqghd', p.astype(v.dtype), v)


def moe_ffn(h, w_rg, b_rg, w_re, b_re, w_gate, w_up, w_down):
    n, d = h.shape
    logit_g = (h @ w_rg).astype(jnp.float32) + b_rg.astype(jnp.float32)
    grp = jnp.argmax(logit_g, axis=-1)
    p_grp = jnp.take_along_axis(jax.nn.softmax(logit_g, axis=-1), grp[:, None], axis=-1)
    logit_e = ((h @ w_re).astype(jnp.float32) + b_re.astype(jnp.float32)).reshape(n, N_GROUPS, EXPERTS_PER_GROUP)
    logit_e = jnp.take_along_axis(logit_e, grp[:, None, None], axis=1)[:, 0]
    top_val, top_idx = lax.top_k(logit_e, TOP_K)
    weight = p_grp * jax.nn.softmax(top_val, axis=-1)
    expert = (grp[:, None] * EXPERTS_PER_GROUP + top_idx).reshape(-1).astype(jnp.int32)
    token = jnp.repeat(jnp.arange(n, dtype=jnp.int32), TOP_K)
    n_assign = n * TOP_K
    one_hot = jax.nn.one_hot(expert, N_EXPERTS, dtype=jnp.int32)
    counts = jnp.sum(one_hot, axis=0)
    rank = jnp.take_along_axis(jnp.cumsum(one_hot, axis=0), expert[:, None], axis=1)[:, 0] - 1
    padded = (counts + MOE_BLOCK - 1) // MOE_BLOCK * MOE_BLOCK
    pad_end = jnp.cumsum(padded)
    dest = (pad_end - padded)[expert] + rank
    n_blocks = -(-n_assign // MOE_BLOCK) + N_EXPERTS
    cap = n_blocks * MOE_BLOCK
    slot_tok = jnp.full((cap,), n, jnp.int32).at[dest].set(token)
    slot_w = jnp.zeros((cap,), h.dtype).at[dest].set(weight.reshape(-1).astype(h.dtype))
    block_expert = jnp.minimum(
        jnp.searchsorted(pad_end, jnp.arange(n_blocks, dtype=jnp.int32) * MOE_BLOCK, side='right'),
        N_EXPERTS - 1)
    h_pad = jnp.concatenate([h, jnp.zeros((1, d), h.dtype)], axis=0)

    def expert_block(args):
        tok, e, w = args
        xb = h_pad[tok]
        y = (jax.nn.silu(xb @ w_gate[e]) * (xb @ w_up[e])) @ w_down[e]
        return y * w[:, None]

    y = lax.map(expert_block, (slot_tok.reshape(n_blocks, MOE_BLOCK), block_expert,
                               slot_w.reshape(n_blocks, MOE_BLOCK)))
    out = jnp.zeros((n + 1, d), h.dtype).at[slot_tok].add(y.reshape(cap, d))
    return out[:n]


def layer_forward(x, pemb, past_kv, past_win, rel_bias, lp):
    (norm_mix, w_in, v_norm, w_spatial, b_spatial, pe_cmp, w_phi1, w_phi2, w_branch, w_out,
     norm_ffn, w_rg, b_rg, w_re, b_re, w_gate, w_up, w_down, norm_ple, w_ple_gate, w_ple_proj) = lp
    b, s, _ = x.shape
    past_len = 0 if past_kv is None else past_kv.shape[1]
    q_pos = past_len + np.arange(s)
    h = rms_norm(x, norm_mix)
    u, v, q, kv, g_nsa, g_merge = jnp.split(h @ w_in, SPLIT_POINTS, axis=-1)
    v_rows = rms_norm(jax.nn.gelu(v), v_norm)
    o_a = chunk_mlp(jax.nn.gelu(u), v_rows, w_spatial, b_spatial)
    q = q.reshape(b, s, N_KV_HEADS, HEADS_PER_KV, HEAD_DIM)
    kv = kv.reshape(b, s, N_KV_SLOTS, N_KV_HEADS, HEAD_DIM)
    kv_new = kv[:, :, :N_CACHE_SLOTS]
    kv_win = kv[:, :, N_CACHE_SLOTS:]
    hist = kv_new if past_kv is None else jnp.concatenate([past_kv, kv_new], axis=1)
    o_cmp, o_slc = cmp_sel_attn(q, hist, q_pos, rel_bias, pe_cmp, w_phi1, w_phi2)
    if past_win is None:
        qb = math.gcd(s, WIN_QBLOCK)
        nb = s // qb
        k_idx = np.arange(nb)[:, None] * qb + np.arange(WINDOW + qb)[None]
        win_pad = jnp.pad(kv_win, ((0, 0), (WINDOW, 0), (0, 0), (0, 0), (0, 0)))
        k_blk = win_pad[:, k_idx]
        k_pos = k_idx - WINDOW
        win_state = kv_win[:, s - min(WINDOW, s):]
    else:
        w_buf = past_win.shape[1]
        rows = jnp.concatenate([past_win, kv_win], axis=1)
        qb, nb = s, 1
        k_blk = rows[:, None]
        k_pos = (past_len - w_buf + np.arange(w_buf + s))[None]
        win_state = rows[:, s:]
    o_win = window_attn(q.reshape(b, nb, qb, N_KV_HEADS, HEADS_PER_KV, HEAD_DIM),
                        k_blk[:, :, :, 0], k_blk[:, :, :, 1], q_pos.reshape(nb, qb), k_pos,
                        rel_bias).reshape(b, s, N_KV_HEADS, HEADS_PER_KV, HEAD_DIM)
    gates = jax.nn.sigmoid(g_nsa).reshape(b, s, 3, N_KV_HEADS, HEADS_PER_KV, 1)
    o_b = (gates[:, :, 0] * o_cmp + gates[:, :, 1] * o_slc + gates[:, :, 2] * o_win).reshape(b, s, BRANCH_WIDTH)
    gm = jax.nn.sigmoid(g_merge).reshape(b, s, 2, D_MODEL)
    merged = gm[:, :, 0] * (o_a @ w_branch[0]) + gm[:, :, 1] * (o_b @ w_branch[1])
    x = x + merged @ w_out
    h2 = rms_norm(x, norm_ffn).reshape(b * s, D_MODEL)
    x = x + moe_ffn(h2, w_rg, b_rg, w_re, b_re, w_gate, w_up, w_down).reshape(b, s, D_MODEL)
    gate = jax.nn.sigmoid(rms_norm(x, norm_ple) @ w_ple_gate)
    x = x + gate * (pemb @ w_ple_proj)
    return x, kv_new, win_state, v_rows


def setup_inputs(seed: int = 0) -> dict:
    key = jax.random.key(seed)
    ks = jax.random.split(key, 32)

    def nrm(k, shape, scale):
        return jax.random.normal(k, shape, jnp.float32) * scale

    n_pages = PAST_LEN // PAGE_SIZE
    n_used = DEC_BATCH * n_pages
    n_pool = n_used + n_used // 4 + 1
    perm = jax.random.permutation(ks[0], n_pool)
    page_table = perm[:n_used].reshape(DEC_BATCH, n_pages).astype(jnp.int32)
    w_buf = min(WINDOW, PAST_LEN)
    return {
        'x_prompt': nrm(ks[1], (BATCH, SEQ, D_MODEL), 1.0),
        'x_sample': nrm(ks[2], (DEC_BATCH, DEC_SEQ, D_MODEL), 1.0),
        'cache_kv': nrm(ks[3], (DEPTH, n_pool, PAGE_SIZE, N_CACHE_SLOTS, N_KV_HEADS, HEAD_DIM), 1.0),
        'state_win_kv': nrm(ks[4], (DEPTH, DEC_BATCH, w_buf, 2, N_KV_HEADS, HEAD_DIM), 1.0),
        'page_table': page_table,
        'p_prompt': nrm(ks[5], (DEPTH, BATCH, SEQ, PLE_DIM), 1.0),
        'p_sample': nrm(ks[6], (DEPTH, DEC_BATCH, DEC_SEQ, PLE_DIM), 1.0),
        'rel_bias': nrm(ks[7], (N_BUCKETS, N_HEADS), 0.5),
        'norm_mix': 1.0 + nrm(ks[8], (DEPTH, D_MODEL), 0.02),
        'w_in': nrm(ks[9], (DEPTH, D_MODEL, IN_COLS), D_MODEL ** -0.5),
        'v_norm': 1.0 + nrm(ks[10], (DEPTH, A_WIDTH), 0.02),
        'w_spatial': nrm(ks[11], (DEPTH, A_GROUPS, CHUNK, CHUNK), CHUNK ** -0.5),
        'b_spatial': 1.0 + nrm(ks[12], (DEPTH, A_GROUPS, CHUNK), 0.1),
        'pe_cmp': nrm(ks[13], (DEPTH, 2, CMP_LEN, HEAD_DIM), 0.1),
        'w_phi1': nrm(ks[14], (DEPTH, 2, CMP_LEN * HEAD_DIM, CMP_HIDDEN), (CMP_LEN * HEAD_DIM) ** -0.5),
        'w_phi2': nrm(ks[15], (DEPTH, 2, CMP_HIDDEN, HEAD_DIM), CMP_HIDDEN ** -0.5),
        'w_branch': nrm(ks[16], (DEPTH, 2, BRANCH_WIDTH, D_MODEL), BRANCH_WIDTH ** -0.5),
        'w_out': nrm(ks[17], (DEPTH, D_MODEL, D_MODEL), D_MODEL ** -0.5),
        'norm_ffn': 1.0 + nrm(ks[18], (DEPTH, D_MODEL), 0.02),
        'w_router_group': nrm(ks[19], (DEPTH, D_MODEL, N_GROUPS), D_MODEL ** -0.5),
        'b_router_group': nrm(ks[20], (DEPTH, N_GROUPS), 0.01),
        'w_router_expert': nrm(ks[21], (DEPTH, D_MODEL, N_EXPERTS), D_MODEL ** -0.5),
        'b_router_expert': nrm(ks[22], (DEPTH, N_EXPERTS), 0.01),
        'w_exp_gate': nrm(ks[23], (DEPTH, N_EXPERTS, D_MODEL, D_EXPERT), D_MODEL ** -0.5),
        'w_exp_up': nrm(ks[24], (DEPTH, N_EXPERTS, D_MODEL, D_EXPERT), D_MODEL ** -0.5),
        'w_exp_down': nrm(ks[25], (DEPTH, N_EXPERTS, D_EXPERT, D_MODEL), D_EXPERT ** -0.5),
        'norm_ple': 1.0 + nrm(ks[26], (DEPTH, D_MODEL), 0.02),
        'w_ple_gate': nrm(ks[27], (DEPTH, D_MODEL, D_MODEL), D_MODEL ** -0.5),
        'w_ple_proj': nrm(ks[28], (DEPTH, PLE_DIM, D_MODEL), PLE_DIM ** -0.5),
        'final_norm': 1.0 + nrm(ks[29], (D_MODEL,), 0.02),
    }


def reference(x_prompt, x_sample, cache_kv, state_win_kv, page_table, p_prompt, p_sample, rel_bias,
              norm_mix, w_in, v_norm, w_spatial, b_spatial, pe_cmp, w_phi1, w_phi2, w_branch, w_out,
              norm_ffn, w_router_group, b_router_group, w_router_expert, b_router_expert,
              w_exp_gate, w_exp_up, w_exp_down, norm_ple, w_ple_gate, w_ple_proj, final_norm):
    x_p, x_s = x_prompt, x_sample
    kv_p, kv_s, win_p, win_s, v_s = [], [], [], [], []
    for i in range(DEPTH):
        lp = (norm_mix[i], w_in[i], v_norm[i], w_spatial[i], b_spatial[i], pe_cmp[i], w_phi1[i], w_phi2[i],
              w_branch[i], w_out[i], norm_ffn[i], w_router_group[i], b_router_group[i],
              w_router_expert[i], b_router_expert[i], w_exp_gate[i], w_exp_up[i], w_exp_down[i],
              norm_ple[i], w_ple_gate[i], w_ple_proj[i])
        x_p, kv_new_p, win_new_p, _ = layer_forward(x_p, p_prompt[i], None, None, rel_bias, lp)
        kv_p.append(kv_new_p)
        win_p.append(win_new_p)
        past_kv = cache_kv[i][page_table].reshape(page_table.shape[0], -1, N_CACHE_SLOTS, N_KV_HEADS, HEAD_DIM)
        x_s, kv_new_s, win_new_s, v_rows_s = layer_forward(x_s, p_sample[i], past_kv, state_win_kv[i], rel_bias, lp)
        kv_s.append(kv_new_s)
        win_s.append(win_new_s)
        v_s.append(v_rows_s)
    y_prompt = rms_norm(x_p, final_norm)
    y_sample = rms_norm(x_s, final_norm)
    kv_rows_prompt = jnp.stack(kv_p)
    kv_rows_sample = jnp.stack(kv_s)
    win_prompt = jnp.stack(win_p)
    win_sample = jnp.stack(win_s)
    chunk_v_sample = jnp.stack(v_s)
    return (y_prompt, y_sample, kv_rows_prompt, kv_rows_sample, win_prompt, win_sample, chunk_v_sample)
```

```python
import functools
import math

import numpy as np
import jax
import jax.numpy as jnp
from jax import lax
from jax.experimental import pallas as pl
from jax.experimental.pallas import tpu as pltpu

F32 = jnp.float32
BF16 = jnp.bfloat16
I32 = jnp.int32

D_MODEL = 1024
A_WIDTH = 1024
A_GROUPS = 4
A_GROUP_WIDTH = A_WIDTH // A_GROUPS
CHUNK = 128
N_HEADS = 16
HEAD_DIM = 64
N_KV = 4
HPG = N_HEADS // N_KV
CMP_LEN = 32
CMP_STRIDE = 16
CMP_HIDDEN = 256
SEL_LEN = 64
SEL_TOP = 16
WINDOW = 512
N_BUCKETS = 32
MAX_DISTANCE = 128
N_GROUPS = 4
EXPERTS_PER_GROUP = 8
N_EXPERTS = 32
D_EXPERT = 512
PLE_DIM = 256
EPS = 1e-6
NEG = -1e30
FORCE = 1e9
NEG_PAD = -3e38

LANES = 128
TOK_TILE = 256
ATT_TILE = 128
MOE_BLOCK = 256
VMEM_LIMIT = 56 * 1024 * 1024


def _dot(a, b):
    return jnp.dot(a, b, preferred_element_type=F32)


def _dot_nt(a, b):
    return lax.dot_general(a, b, (((1,), (1,)), ((), ())), preferred_element_type=F32)


def _dot_tn(a, b):
    return lax.dot_general(a, b, (((0,), (0,)), ((), ())), preferred_element_type=F32)


def _hilo(a):
    hi = a.astype(BF16)
    lo = (a - hi.astype(F32)).astype(BF16)
    return hi, lo


def _dot_hilo_l(a, b):
    hi, lo = _hilo(a)
    return _dot(hi, b) + _dot(lo, b)


def _gelu(x):
    return 0.5 * x * (1.0 + jnp.tanh(0.7978845608028654 * (x + 0.044715 * (x * x * x))))


def _sigmoid(x):
    return 1.0 / (1.0 + jnp.exp(-x))


def _rms(x, gain):
    return x * lax.rsqrt(jnp.mean(x * x, axis=-1, keepdims=True) + EPS) * gain


def _full(shape):
    nd = len(shape)
    return pl.BlockSpec(shape, lambda *_: (0,) * nd)


def _params(sem, vmem=VMEM_LIMIT):
    return pltpu.CompilerParams(dimension_semantics=sem, vmem_limit_bytes=vmem)


def _bucket_np(dist):
    n = np.maximum(np.asarray(dist, np.int64), 0)
    max_exact = N_BUCKETS // 2
    nf = np.maximum(n, max_exact).astype(np.float64)
    large = max_exact + (np.log(nf / max_exact) / math.log(MAX_DISTANCE / max_exact)
                         * (N_BUCKETS - max_exact)).astype(np.int64)
    return np.where(n < max_exact, n, np.minimum(large, N_BUCKETS - 1)).astype(np.int32)


def _inproj_kernel(x_ref, g_ref, wu, wv, wq, wkc, wkw, wgn, wgm, vn_ref,
                   gu_o, v_o, q_o, kvc_o, kvw_o, gn_o, gm_o):
    x = x_ref[...]
    hb = _rms(x, g_ref[...]).astype(BF16)
    gu_o[...] = _gelu(_dot(hb, wu[...])).astype(BF16)
    v_o[...] = _rms(_gelu(_dot(hb, wv[...])), vn_ref[...])
    q_o[...] = (_dot(hb, wq[...]) * (HEAD_DIM ** -0.5)).astype(BF16)
    kvc_o[...] = _dot(hb, wkc[...])
    kvw_o[...] = _dot(hb, wkw[...])
    gn_o[...] = _sigmoid(_dot(hb, wgn[...]))
    gm_o[...] = _sigmoid(_dot(hb, wgm[...])).astype(BF16)


def _inproj(x, gain, w_in, v_gain):
    t = x.shape[0]
    tm = TOK_TILE
    a = A_WIDTH
    c_q, c_kv, c_gn, c_gm = 2 * a, 3 * a, 3 * a + 1536, 3 * a + 1536 + 48
    wb = w_in.astype(BF16)
    wu, wv, wq = wb[:, :a], wb[:, a:2 * a], wb[:, c_q:c_kv]
    wkc, wkw = wb[:, c_kv:c_kv + 1024], wb[:, c_kv + 1024:c_gn]
    wgn = jnp.pad(wb[:, c_gn:c_gm], ((0, 0), (0, LANES - 48)))
    wgm = wb[:, c_gm:]
    row = lambda n: pl.BlockSpec((tm, n), lambda i: (i, 0))
    outs = [(a, BF16), (a, F32), (a, BF16), (1024, F32), (512, F32), (LANES, F32), (2 * D_MODEL, BF16)]
    return pl.pallas_call(
        _inproj_kernel,
        grid=(t // tm,),
        in_specs=[row(D_MODEL), _full((1, D_MODEL)), _full(wu.shape), _full(wv.shape), _full(wq.shape),
                  _full(wkc.shape), _full(wkw.shape), _full(wgn.shape), _full(wgm.shape), _full((1, a))],
        out_specs=[row(n) for n, _ in outs],
        out_shape=[jax.ShapeDtypeStruct((t, n), d) for n, d in outs],
        compiler_params=_params(("parallel",)),
        name="inproj",
    )(x, gain.reshape(1, -1), wu, wv, wq, wkc, wkw, wgn, wgm, v_gain.reshape(1, -1))


def _cmp_prompt_kernel(h_ref, pe_ref, w1_ref, w2_ref, o_ref, pt_ref):
    g, nhb = h_ref.shape[2], h_ref.shape[3]
    h = h_ref[0, 0].reshape(g * nhb, CMP_LEN // 2 * HEAD_DIM)
    w1 = w1_ref[0]
    half = w1.shape[0] // 2
    pe_term = _dot(pe_ref[0], w1)
    p = _dot(h, w1[:half])
    q = _dot(h, w1[half:])
    pre = p + pltpu.roll(q, g * nhb - 1, 0) + pe_term[0:1]
    o = _dot(_gelu(pre).astype(BF16), w2_ref[0])
    o_ref[0, 0] = o.reshape(g, nhb, HEAD_DIM).astype(BF16)
    pt_ref[0] = pe_term


def _cmp_prompt(kh, pe_flat, w1, w2):
    _, b, g, nhb, k = kh.shape
    return pl.pallas_call(
        _cmp_prompt_kernel,
        grid=(2, b),
        in_specs=[pl.BlockSpec((1, 1, g, nhb, k), lambda s, i: (s, i, 0, 0, 0)),
                  pl.BlockSpec((1, 8, 2 * k), lambda s, i: (s, 0, 0)),
                  pl.BlockSpec((1, 2 * k, CMP_HIDDEN), lambda s, i: (s, 0, 0)),
                  pl.BlockSpec((1, CMP_HIDDEN, HEAD_DIM), lambda s, i: (s, 0, 0))],
        out_specs=[pl.BlockSpec((1, 1, g, nhb, HEAD_DIM), lambda s, i: (s, i, 0, 0, 0)),
                   pl.BlockSpec((1, 8, CMP_HIDDEN), lambda s, i: (s, 0, 0))],
        out_shape=[jax.ShapeDtypeStruct((2, b, g, nhb, HEAD_DIM), BF16),
                   jax.ShapeDtypeStruct((2, 8, CMP_HIDDEN), F32)],
        compiler_params=_params(("arbitrary", "arbitrary")),
        name="cmp_prompt",
    )(kh, pe_flat, w1, w2)


def _top_blocks(score, n_sel, n_top):
    row = lax.broadcasted_iota(I32, score.shape, 0)
    cnt = jnp.zeros(score.shape, F32)
    for j in range(n_sel):
        sj = score[j:j + 1, :]
        beats = jnp.where(sj > score, 1.0, jnp.where(sj == score, jnp.where(row > j, 1.0, 0.0), 0.0))
        cnt = cnt + beats
    return jnp.where((cnt < n_top) & (row < n_sel), 1.0, 0.0)


def _nsa_prompt_kernel(q_ref, kc_ref, vc_ref, ks_ref, vs_ref, kw_ref, vw_ref, bc_ref, bt_ref, ovt_ref, e_ref,
                       o_ref, selx_ref, m_ref, l_ref, acc_ref, *, n_cmp, n_sel, n_top):
    tq = ATT_TILE
    tk = ATT_TILE
    rows = HPG * tq
    i = pl.program_id(2)
    q = q_ref[0, 0].reshape(rows, HEAD_DIM)

    s = _dot_nt(q, kc_ref[0, 0, 0]) + bc_ref[0].reshape(rows, LANES)
    r_io = lax.broadcasted_iota(I32, (rows, LANES), 0)
    c_io = lax.broadcasted_iota(I32, (rows, LANES), 1)
    qpos = i * tq + (r_io & (tq - 1))
    valid = (qpos >= c_io * CMP_STRIDE + (CMP_LEN - 1)) & (c_io < n_cmp)
    s = jnp.where(valid, s, NEG)
    e = jnp.exp(s - jnp.max(s, axis=-1, keepdims=True))
    p = jnp.where(valid, e / jnp.sum(e, axis=-1, keepdims=True), 0.0)
    o_ref[0, 0, 0] = _dot(p.astype(BF16), vc_ref[0, 0, 0]).reshape(HPG, tq, HEAD_DIM).astype(BF16)

    psum = p[0:tq] + p[tq:2 * tq] + p[2 * tq:3 * tq] + p[3 * tq:4 * tq]
    p_hi, p_lo = _hilo(psum)
    imp = _dot_nt(ovt_ref[...], p_hi) + _dot_nt(ovt_ref[...], p_lo)
    j_io = lax.broadcasted_iota(I32, (LANES, tq), 0)
    qp2 = i * tq + lax.broadcasted_iota(I32, (LANES, tq), 1)
    qblk = lax.shift_right_logical(qp2, int(math.log2(SEL_LEN)))
    forced = (j_io == 0) | (j_io == qblk) | (j_io == qblk - 1)
    score = jnp.where(forced, FORCE, jnp.where(j_io <= qblk, imp, NEG))
    score = jnp.where(j_io < n_sel, score, NEG_PAD)
    sel_t = _top_blocks(score, n_sel, n_top)
    sel = sel_t.T.astype(BF16)
    selx_ref[...] = _dot(sel, e_ref[...])

    kl = lax.broadcasted_iota(I32, (rows, tk), 1)
    ql = r_io & (tq - 1)

    def first_tile(k_ref, v_ref, kt, bias, mask):
        k = k_ref[0, 0, 0, pl.ds(pl.multiple_of(kt * tk, tk), tk), :]
        v = v_ref[0, 0, 0, pl.ds(pl.multiple_of(kt * tk, tk), tk), :]
        s = jnp.where(mask, _dot_nt(q, k) + bias, NEG)
        m = jnp.max(s, axis=-1, keepdims=True)
        pp = jnp.exp(s - m)
        m_ref[...] = m
        l_ref[...] = jnp.sum(pp, axis=-1, keepdims=True)
        acc_ref[...] = _dot(pp.astype(BF16), v)

    def next_tile(k_ref, v_ref, kt, bias, mask):
        k = k_ref[0, 0, 0, pl.ds(pl.multiple_of(kt * tk, tk), tk), :]
        v = v_ref[0, 0, 0, pl.ds(pl.multiple_of(kt * tk, tk), tk), :]
        s = _dot_nt(q, k) + bias
        if mask is not None:
            s = jnp.where(mask, s, NEG)
        m_prev = m_ref[...]
        m_new = jnp.maximum(m_prev, jnp.max(s, axis=-1, keepdims=True))
        alpha = jnp.exp(m_prev - m_new)
        pp = jnp.exp(s - m_new)
        l_ref[...] = alpha * l_ref[...] + jnp.sum(pp, axis=-1, keepdims=True)
        acc_ref[...] = alpha * acc_ref[...] + _dot(pp.astype(BF16), v)
        m_ref[...] = m_new

    def sel_mask(kt):
        sx = selx_ref[:, pl.ds(pl.multiple_of(kt * tk, tk), tk)]
        return jnp.concatenate([sx] * HPG, axis=0) > 0.5

    def finish(branch):
        o_ref[branch, 0, 0] = (acc_ref[...] / l_ref[...]).reshape(HPG, tq, HEAD_DIM).astype(BF16)

    first_tile(ks_ref, vs_ref, i, bt_ref[0, 0], sel_mask(i) & (kl <= ql))

    @pl.when(i >= 1)
    def _():
        next_tile(ks_ref, vs_ref, i - 1, bt_ref[0, 1], sel_mask(i - 1))

    def far_slc(kt, c):
        next_tile(ks_ref, vs_ref, kt, bt_ref[0, 2], sel_mask(kt))
        return c

    lax.fori_loop(0, jnp.maximum(i - 1, 0), far_slc, 0)
    finish(1)

    n_win = WINDOW // tk
    first_tile(kw_ref, vw_ref, i, bt_ref[0, 0], kl <= ql)

    @pl.when(i >= 1)
    def _():
        next_tile(kw_ref, vw_ref, i - 1, bt_ref[0, 1], None)

    def far_win(kt, c):
        next_tile(kw_ref, vw_ref, kt, bt_ref[0, 2], None)
        return c

    lax.fori_loop(jnp.maximum(i - n_win + 1, 0), jnp.maximum(i - 1, 0), far_win, 0)

    @pl.when(i >= n_win)
    def _():
        next_tile(kw_ref, vw_ref, i - n_win, bt_ref[0, 2], kl > ql)

    finish(2)


def _nsa_prompt(q5, kcv, kv4, kvw2, bias_cmp, btile, ovt, emat):
    b, g, _, s, _ = q5.shape
    tq = ATT_TILE
    n_cmp = (s - CMP_LEN) // CMP_STRIDE + 1
    n_sel = -(-s // SEL_LEN)
    n_top = min(SEL_TOP, n_sel)
    nhb = kcv.shape[3]
    assert nhb == LANES and s % tq == 0 and WINDOW % tq == 0
    kvspec = lambda slot: pl.BlockSpec((1, 1, 1, s, HEAD_DIM), lambda bi, gi, i: (slot, bi, gi, 0, 0))
    cspec = lambda slot: pl.BlockSpec((1, 1, 1, nhb, HEAD_DIM), lambda bi, gi, i: (slot, bi, gi, 0, 0))
    rows = HPG * tq
    return pl.pallas_call(
        functools.partial(_nsa_prompt_kernel, n_cmp=n_cmp, n_sel=n_sel, n_top=n_top),
        grid=(b, g, s // tq),
        in_specs=[pl.BlockSpec((1, 1, HPG, tq, HEAD_DIM), lambda bi, gi, i: (bi, gi, 0, i, 0)),
                  cspec(0), cspec(1), kvspec(2), kvspec(3), kvspec(0), kvspec(1),
                  pl.BlockSpec((1, HPG, tq, LANES), lambda bi, gi, i: (gi, 0, i, 0)),
                  pl.BlockSpec((1, 3, rows, tq), lambda bi, gi, i: (gi, 0, 0, 0)),
                  _full(ovt.shape), _full(emat.shape)],
        out_specs=pl.BlockSpec((3, 1, 1, HPG, tq, HEAD_DIM), lambda bi, gi, i: (0, bi, gi, 0, i, 0)),
        out_shape=jax.ShapeDtypeStruct((3, b, g, HPG, s, HEAD_DIM), BF16),
        scratch_shapes=[pltpu.VMEM((tq, s), F32), pltpu.VMEM((rows, 1), F32), pltpu.VMEM((rows, 1), F32),
                        pltpu.VMEM((rows, HEAD_DIM), F32)],
        compiler_params=_params(("parallel", "parallel", "arbitrary")),
        name="nsa_prompt",
    )(q5, kcv, kcv, kv4, kv4, kvw2, kvw2, bias_cmp, btile, ovt, emat)


def _nsa_sample_kernel(pt_ref, *refs, n_pages, n_sel, n_top):
    pages = refs[:n_pages]
    (qr_ref, kvcn_ref, kvwn_ref, swin_ref, wq_ref, pet_ref, w2p_ref, bct_ref, bst_ref, bwt_ref, mwt_ref, mnt_ref,
     ovt_ref, hsum_ref, o_ref, win_ref, s_ref) = refs[n_pages:]
    del pt_ref
    gd = N_KV * HEAD_DIM
    nrow = n_pages * (LANES // CMP_STRIDE)
    qr = qr_ref[0]
    n_chunk = pages[0].shape[2] // LANES

    def page_cols(pg, slot):
        return jnp.concatenate([pg[0, 0, pl.ds(2 * slot + h, LANES, stride=n_chunk), :] for h in range(2)], axis=1)

    kc_all = []
    for slot in range(2):
        out = jnp.zeros((nrow, gd), F32)
        for pair in range(2):
            lane0 = slot * gd + pair * LANES
            acc = jnp.zeros((nrow, 4 * CMP_HIDDEN), F32)
            for r2 in range(CMP_STRIDE // 2):
                taps = []
                for r in (2 * r2, 2 * r2 + 1):
                    taps.append(jnp.concatenate(
                        [pg[0, 0, pl.ds(r * n_chunk + lane0 // LANES, LANES // CMP_STRIDE, stride=CMP_STRIDE * n_chunk), :]
                         for pg in pages], axis=0))
                lhs = jnp.concatenate(taps, axis=1).astype(BF16)
                acc = acc + _dot(lhs, wq_ref[slot, r2 * 2 * LANES:(r2 + 1) * 2 * LANES, :])
            for gi in range(2):
                c0 = gi * 2 * CMP_HIDDEN
                pre = (acc[:, c0:c0 + CMP_HIDDEN]
                       + pltpu.roll(acc[:, c0 + CMP_HIDDEN:c0 + 2 * CMP_HIDDEN], nrow - 1, 0)
                       + pet_ref[slot, 0:1])
                out = out + _dot(_gelu(pre).astype(BF16), w2p_ref[slot, 2 * pair + gi])
        kc_all.append(out.astype(BF16))

    cols = LANES

    def softmax_t(s):
        m = jnp.max(s, axis=0, keepdims=True)
        e = jnp.exp(s - m)
        return e / jnp.sum(e, axis=0, keepdims=True)

    def as_column(row):
        return jnp.broadcast_to(row, (cols, cols)).T[:, 0:1]

    def diag_blocks(o):
        return jnp.concatenate([o[g * 32:(g + 1) * 32, g * HEAD_DIM:(g + 1) * HEAD_DIM] for g in range(N_KV)],
                               axis=0)

    n_cmp = nrow - 1
    c_io = lax.broadcasted_iota(I32, (nrow, cols), 0)
    s = _dot_nt(kc_all[0], qr) + bct_ref[...]
    valid = c_io < n_cmp
    p = jnp.where(valid, softmax_t(jnp.where(valid, s, NEG)), 0.0)
    o_ref[0, 0] = diag_blocks(_dot_tn(p.astype(BF16), kc_all[1])).astype(BF16)

    pg_sum = _dot_hilo_l(p, hsum_ref[...])
    g_hi, g_lo = _hilo(pg_sum)
    imp = _dot(ovt_ref[...], g_hi) + _dot(ovt_ref[...], g_lo)
    j_io = lax.broadcasted_iota(I32, imp.shape, 0)
    qblk = n_sel - 1
    forced = (j_io == 0) | (j_io == qblk) | (j_io == qblk - 1)
    score = jnp.where(forced, FORCE, jnp.where(j_io <= qblk, imp, NEG))
    score = jnp.where(j_io < n_sel, score, NEG_PAD)
    sel = _top_blocks(score, n_sel, n_top)

    per_page = LANES // SEL_LEN
    for k, pg in enumerate(pages):
        kp = page_cols(pg, 2).astype(BF16)
        sk = _dot_nt(kp, qr) + bst_ref[k * LANES:(k + 1) * LANES, :]
        mk = jnp.concatenate([jnp.broadcast_to(sel[per_page * k + t:per_page * k + t + 1, :], (SEL_LEN, cols))
                              for t in range(per_page)], axis=0)
        s_ref[k * LANES:(k + 1) * LANES, :] = jnp.where(mk > 0.5, sk, NEG)
    zpad = jnp.zeros((LANES - 8, gd), F32)
    kn = jnp.concatenate([kvcn_ref[0][:, 2 * gd:3 * gd], zpad], axis=0).astype(BF16)
    vn = jnp.concatenate([kvcn_ref[0][:, 3 * gd:4 * gd], zpad], axis=0).astype(BF16)
    sn = _dot_nt(kn, qr) + bst_ref[n_pages * LANES:(n_pages + 1) * LANES, :]
    mn = (mnt_ref[...] > 0.5) & (jnp.broadcast_to(sel[n_sel - 1:n_sel, :], (LANES, cols)) > 0.5)
    s_ref[n_pages * LANES:(n_pages + 1) * LANES, :] = jnp.where(mn, sn, NEG)
    n_keys = (n_pages + 1) * LANES
    m = jnp.max(s_ref[0:n_keys, :], axis=0, keepdims=True)
    den = jnp.zeros((1, cols), F32)
    o = jnp.zeros((cols, gd), F32)
    for k in range(n_pages + 1):
        pk = jnp.exp(s_ref[k * LANES:(k + 1) * LANES, :] - m)
        den = den + jnp.sum(pk, axis=0, keepdims=True)
        vk = vn if k == n_pages else page_cols(pages[k], 3).astype(BF16)
        o = o + _dot_tn(pk.astype(BF16), vk)
    o_ref[1, 0] = diag_blocks(o / as_column(den)).astype(BF16)

    w_buf = swin_ref.shape[2]
    kw = swin_ref[0, 0][:, 0:gd].astype(BF16)
    vw = swin_ref[0, 0][:, gd:2 * gd].astype(BF16)
    kwn = jnp.concatenate([kvwn_ref[0][:, 0:gd], zpad], axis=0).astype(BF16)
    vwn = jnp.concatenate([kvwn_ref[0][:, gd:2 * gd], zpad], axis=0).astype(BF16)
    sw = jnp.concatenate([_dot_nt(kw, qr), _dot_nt(kwn, qr)], axis=0) + bwt_ref[...]
    sw = jnp.where(mwt_ref[...] > 0.5, sw, NEG)
    mw = jnp.max(sw, axis=0, keepdims=True)
    pw = jnp.exp(sw - mw)
    denw = jnp.sum(pw, axis=0, keepdims=True)
    ow = _dot_tn(pw[0:w_buf].astype(BF16), vw) + _dot_tn(pw[w_buf:].astype(BF16), vwn)
    o_ref[2, 0] = diag_blocks(ow / as_column(denw)).astype(BF16)

    n_new = kvwn_ref.shape[1]
    win_ref[0, 0, 0:w_buf - n_new, :] = swin_ref[0, 0, n_new:w_buf, :]
    win_ref[0, 0, w_buf - n_new:w_buf, :] = kvwn_ref[0]


def _nsa_sample(layer, win_prev, page_flat, cache4, qrows, kvcn, kvwn, swin, wq, pet, w2p,
                bct, bst, bwt, mwt, mnt, ovt, hsum, n_sel):
    bs = qrows.shape[0]
    depth = swin.shape[0]
    n_pages = page_flat.shape[0] // bs
    w_buf = swin.shape[2]
    n_top = min(SEL_TOP, n_sel)
    page_specs = [pl.BlockSpec((1, 1, cache4.shape[2], LANES),
                               functools.partial(lambda b, pt, k: (layer, pt[b * n_pages + k], 0, 0), k=k))
                  for k in range(n_pages)]
    cfull = lambda a: pl.BlockSpec(a.shape, lambda b, pt: (0,) * a.ndim)
    in_specs = page_specs + [
        pl.BlockSpec((1, LANES, N_KV * HEAD_DIM), lambda b, pt: (b, 0, 0)),
        pl.BlockSpec((1,) + kvcn.shape[1:], lambda b, pt: (b, 0, 0)),
        pl.BlockSpec((1,) + kvwn.shape[1:], lambda b, pt: (b, 0, 0)),
        pl.BlockSpec((1, 1, w_buf, swin.shape[3]), lambda b, pt: (layer, b, 0, 0)),
        cfull(wq), cfull(pet), cfull(w2p), cfull(bct), cfull(bst), cfull(bwt), cfull(mwt), cfull(mnt),
        cfull(ovt), cfull(hsum)]
    args = [cache4] * n_pages + [qrows, kvcn, kvwn, swin, wq, pet, w2p, bct, bst, bwt, mwt, mnt, ovt, hsum]
    in_specs.append(pl.BlockSpec(memory_space=pl.ANY))
    args.append(win_prev)
    aliases = {len(args): 1}

    def body(*refs):
        refs = refs[:len(in_specs)] + refs[len(in_specs) + 1:]
        _nsa_sample_kernel(*refs, n_pages=n_pages, n_sel=n_sel, n_top=n_top)

    grid_spec = pltpu.PrefetchScalarGridSpec(
        num_scalar_prefetch=1, grid=(bs,), in_specs=in_specs,
        out_specs=[pl.BlockSpec((3, 1, LANES, HEAD_DIM), lambda b, pt: (0, b, 0, 0)),
                   pl.BlockSpec((1, 1, w_buf, swin.shape[3]), lambda b, pt: (layer, b, 0, 0))],
        scratch_shapes=[pltpu.VMEM(((n_pages + 1) * LANES, LANES), F32)])
    return pl.pallas_call(
        body, grid_spec=grid_spec,
        out_shape=[jax.ShapeDtypeStruct((3, bs, LANES, HEAD_DIM), BF16),
                   jax.ShapeDtypeStruct((depth, bs, w_buf, swin.shape[3]), F32)],
        input_output_aliases=aliases,
        compiler_params=_params(("arbitrary",)),
        name="nsa_sample",
    )(page_flat, *args)


def _merge_kernel(x_ref, gu_ref, v_ref, o3_ref, gn_ref, gm_ref, wmix_ref, bmix_ref, ex_ref, wb_ref, wo_ref,
                  nf_ref, wrh_ref, wrl_ref, br_ref, x1_o, h2_o, ei_o, rw_o):
    tm = x_ref.shape[0]
    mixed = []
    for c in range(tm // CHUNK):
        vb = v_ref[c * CHUNK:(c + 1) * CHUNK, :].astype(BF16)
        mixed.append(jnp.concatenate(
            [_dot(wmix_ref[0, g], vb[:, g * A_GROUP_WIDTH:(g + 1) * A_GROUP_WIDTH]) for g in range(A_GROUPS)],
            axis=1) + bmix_ref[0])
    o_a = gu_ref[...].astype(F32) * jnp.concatenate(mixed, axis=0)
    gn = gn_ref[...]
    o_b = jnp.zeros((tm, D_MODEL), F32)
    for br in range(3):
        o_b = o_b + _dot_hilo_l(gn, ex_ref[br]) * o3_ref[br].astype(F32)
    gm = gm_ref[...].astype(F32)
    merged = (gm[:, :D_MODEL] * _dot(o_a.astype(BF16), wb_ref[0])
              + gm[:, D_MODEL:] * _dot(o_b.astype(BF16), wb_ref[1]))
    x1 = x_ref[...] + _dot(merged.astype(BF16), wo_ref[...])
    x1_o[...] = x1
    h2 = _rms(x1, nf_ref[...])
    h2_o[...] = h2
    hh, hl = _hilo(h2)
    logit = _dot(hh, wrh_ref[...]) + _dot(hl, wrh_ref[...]) + _dot(hh, wrl_ref[...]) + br_ref[...]
    lane = lax.broadcasted_iota(I32, logit.shape, 1)
    big = jnp.int32(9999)
    is_g = lane < N_GROUPS
    gl = jnp.where(is_g, logit, -jnp.inf)
    gmax = jnp.max(gl, axis=-1, keepdims=True)
    grp = jnp.min(jnp.where(gl == gmax, lane, big), axis=-1, keepdims=True)
    p_grp = 1.0 / jnp.sum(jnp.where(is_g, jnp.exp(logit - gmax), 0.0), axis=-1, keepdims=True)
    e_lane = lane - N_GROUPS
    in_grp = (e_lane >= 0) & (lax.shift_right_arithmetic(e_lane, 3) == grp) & (e_lane < N_EXPERTS)
    el = jnp.where(in_grp, logit, -jnp.inf)
    t1 = jnp.max(el, axis=-1, keepdims=True)
    i1 = jnp.min(jnp.where(el == t1, lane, big), axis=-1, keepdims=True)
    el2 = jnp.where(lane == i1, -jnp.inf, el)
    t2 = jnp.max(el2, axis=-1, keepdims=True)
    i2 = jnp.min(jnp.where(el2 == t2, lane, big), axis=-1, keepdims=True)
    r = jnp.exp(t2 - t1)
    w1 = p_grp / (1.0 + r)
    w2 = p_grp * r / (1.0 + r)
    ei_o[...] = jnp.where(lane == 0, i1 - N_GROUPS, jnp.where(lane == 1, i2 - N_GROUPS, 0))
    rw_o[...] = jnp.where(lane == 0, w1, jnp.where(lane == 1, w2, 0.0))


def _merge(x, gu, v, o3, gn, gm, wmix, bmix, ex, wb, wo, nf, wrh, wrl, brr, n_prompt_tiles):
    t = x.shape[0]
    tm = TOK_TILE
    row = lambda n: pl.BlockSpec((tm, n), lambda i: (i, 0))
    kind = lambda i: jnp.where(i < n_prompt_tiles, 0, 1)
    outs = [(D_MODEL, F32), (D_MODEL, F32), (LANES, I32), (LANES, F32)]
    return pl.pallas_call(
        _merge_kernel,
        grid=(t // tm,),
        in_specs=[row(D_MODEL), row(A_WIDTH), row(A_WIDTH),
                  pl.BlockSpec((3, tm, D_MODEL), lambda i: (0, i, 0)),
                  row(LANES), row(2 * D_MODEL),
                  pl.BlockSpec((1, A_GROUPS, CHUNK, CHUNK), lambda i: (kind(i), 0, 0, 0)),
                  pl.BlockSpec((1, CHUNK, A_WIDTH), lambda i: (kind(i), 0, 0)),
                  _full(ex.shape), _full(wb.shape), _full(wo.shape), _full((1, D_MODEL)),
                  _full(wrh.shape), _full(wrl.shape), _full((1, LANES))],
        out_specs=[row(n) for n, _ in outs],
        out_shape=[jax.ShapeDtypeStruct((t, n), d) for n, d in outs],
        compiler_params=_params(("parallel",)),
        name="merge",
    )(x, gu, v, o3, gn, gm, wmix, bmix, ex, wb, wo, nf.reshape(1, -1), wrh, wrl, brr)


def _route1_kernel(ei_ref, ltri_ref, rank_o, cnt_o, carry):
    @pl.when(pl.program_id(0) == 0)
    def _():
        carry[...] = jnp.zeros(carry.shape, F32)

    ei = ei_ref[...]
    lane = lax.broadcasted_iota(I32, ei.shape, 1)
    e1, e2 = ei[:, 0:1], ei[:, 1:2]
    oh = jnp.where((lane == e1) | (lane == e2), 1.0, 0.0)
    cum = _dot(ltri_ref[...], oh.astype(BF16)) + carry[0:1, :]
    r1 = jnp.sum(jnp.where(lane == e1, cum, 0.0), axis=-1, keepdims=True)
    r2 = jnp.sum(jnp.where(lane == e2, cum, 0.0), axis=-1, keepdims=True)
    rank_o[...] = jnp.where(lane == 0, r1, jnp.where(lane == 1, r2, 0.0)).astype(I32)
    carry[...] = carry[...] + jnp.sum(oh, axis=0, keepdims=True)
    cnt_o[...] = carry[...]


def _route2_kernel(ei_ref, rank_ref, cnt_ref, utri_ref, dest_o, be_o, nu_o):
    shift = int(math.log2(MOE_BLOCK))
    nb = lax.shift_right_logical(cnt_ref[...].astype(I32) + (MOE_BLOCK - 1), shift).astype(F32)
    start = _dot(nb.astype(BF16), utri_ref[...])
    ei = ei_ref[...]
    lane = lax.broadcasted_iota(I32, ei.shape, 1)
    e1, e2 = ei[:, 0:1], ei[:, 1:2]
    s1 = jnp.sum(jnp.where(lane == e1, start[0:1, :], 0.0), axis=-1, keepdims=True)
    s2 = jnp.sum(jnp.where(lane == e2, start[0:1, :], 0.0), axis=-1, keepdims=True)
    rk = rank_ref[...]
    d1 = s1.astype(I32) * MOE_BLOCK + rk[:, 0:1]
    d2 = s2.astype(I32) * MOE_BLOCK + rk[:, 1:2]
    dest_o[...] = jnp.where(lane == 0, d1, jnp.where(lane == 1, d2, 0))
    end = start[0:1, :] + nb[0:1, :]
    j = lax.broadcasted_iota(I32, be_o.shape, 0).astype(F32)
    l2 = lax.broadcasted_iota(I32, be_o.shape, 1)
    ge = jnp.sum(jnp.where((l2 < N_EXPERTS) & (end <= j), 1.0, 0.0), axis=-1, keepdims=True)
    be_o[...] = jnp.broadcast_to(jnp.minimum(ge, N_EXPERTS - 1.0).astype(I32), be_o.shape)
    l3 = lax.broadcasted_iota(I32, nu_o.shape, 1)
    nu = jnp.sum(jnp.where(l3 == N_EXPERTS - 1, jnp.broadcast_to(end, nu_o.shape), 0.0), axis=-1, keepdims=True)
    nu_o[...] = jnp.broadcast_to(nu.astype(I32), nu_o.shape)


def _route(ei, n_blocks):
    t = ei.shape[0]
    tm = TOK_TILE
    row = pl.BlockSpec((tm, LANES), lambda i: (i, 0))
    ltri = jnp.asarray(np.tril(np.ones((tm, tm), np.float32), -1), BF16)
    utri = jnp.asarray(np.triu(np.ones((LANES, LANES), np.float32), 1), BF16)
    rank, cnt = pl.pallas_call(
        _route1_kernel, grid=(t // tm,),
        in_specs=[row, _full((tm, tm))],
        out_specs=[row, _full((8, LANES))],
        out_shape=[jax.ShapeDtypeStruct((t, LANES), I32), jax.ShapeDtypeStruct((8, LANES), F32)],
        scratch_shapes=[pltpu.VMEM((8, LANES), F32)],
        compiler_params=_params(("arbitrary",)),
        name="route_rank",
    )(ei, ltri)
    nbp = -(-n_blocks // 8) * 8
    dest, be, nu = pl.pallas_call(
        _route2_kernel, grid=(t // tm,),
        in_specs=[row, row, _full((8, LANES)), _full((LANES, LANES))],
        out_specs=[row, _full((nbp, LANES)), _full((8, LANES))],
        out_shape=[jax.ShapeDtypeStruct((t, LANES), I32), jax.ShapeDtypeStruct((nbp, LANES), I32),
                   jax.ShapeDtypeStruct((8, LANES), I32)],
        compiler_params=_params(("arbitrary",)),
        name="route_slots",
    )(ei, rank, cnt, utri)
    return dest[:, :2].reshape(-1), be[:n_blocks, 0], nu[0, :1]


def _dispatch_kernel(dest_ref, h_hbm, xs_in, xs_out, sem, *, chunk):
    del xs_in
    i = pl.program_id(0)
    n = pl.num_programs(0)

    def row_copy(t, d):
        return pltpu.make_async_copy(h_hbm.at[pl.ds(t, 1)], xs_out.at[pl.ds(d, 1)], sem)

    def issue(r, c):
        t = i * chunk + r
        row_copy(t, dest_ref[2 * t]).start()
        row_copy(t, dest_ref[2 * t + 1]).start()
        return c

    def drain(r, c):
        row_copy(0, 0).wait()
        row_copy(0, 0).wait()
        return c

    lax.fori_loop(0, chunk, issue, 0)

    @pl.when(i > 0)
    def _():
        lax.fori_loop(0, chunk, drain, 0)

    @pl.when(i == n - 1)
    def _():
        lax.fori_loop(0, chunk, drain, 0)


def _dispatch(dest_flat, h2, cap):
    t, d = h2.shape
    chunk = TOK_TILE
    grid_spec = pltpu.PrefetchScalarGridSpec(
        num_scalar_prefetch=1, grid=(t // chunk,),
        in_specs=[pl.BlockSpec(memory_space=pl.ANY), pl.BlockSpec(memory_space=pl.ANY)],
        out_specs=pl.BlockSpec(memory_space=pl.ANY),
        scratch_shapes=[pltpu.SemaphoreType.DMA(())])
    return pl.pallas_call(
        functools.partial(_dispatch_kernel, chunk=chunk), grid_spec=grid_spec,
        out_shape=jax.ShapeDtypeStruct((cap, d), h2.dtype),
        input_output_aliases={2: 0},
        compiler_params=_params(("arbitrary",)),
        name="moe_dispatch",
    )(dest_flat, h2, jnp.zeros((cap, d), h2.dtype))


def _expert_kernel(be_ref, nu_ref, x_ref, wg_ref, wu_ref, wd_ref, y_ref):
    i = pl.program_id(0)

    @pl.when(i < nu_ref[0])
    def _():
        xb = x_ref[...].astype(BF16)
        g = _dot(xb, wg_ref[0])
        u = _dot(xb, wu_ref[0])
        a = (g * _sigmoid(g) * u).astype(BF16)
        y_ref[...] = _dot(a, wd_ref[0])

    @pl.when(i >= nu_ref[0])
    def _():
        y_ref[...] = jnp.zeros(y_ref.shape, F32)


def _experts(be, nu, xs, wg, wu, wd):
    cap, d = xs.shape
    bm = MOE_BLOCK
    xin = lambda i, be, nu: (jnp.minimum(i, jnp.maximum(nu[0] - 1, 0)), 0)
    wsel = lambda i, be, nu: (be[i], 0, 0)
    grid_spec = pltpu.PrefetchScalarGridSpec(
        num_scalar_prefetch=2, grid=(cap // bm,),
        in_specs=[pl.BlockSpec((bm, d), xin),
                  pl.BlockSpec((1, d, D_EXPERT), wsel), pl.BlockSpec((1, d, D_EXPERT), wsel),
                  pl.BlockSpec((1, D_EXPERT, d), wsel)],
        out_specs=pl.BlockSpec((bm, d), lambda i, be, nu: (i, 0)))
    return pl.pallas_call(
        _expert_kernel, grid_spec=grid_spec,
        out_shape=jax.ShapeDtypeStruct((cap, d), F32),
        compiler_params=_params(("arbitrary",)),
        name="moe_experts",
    )(be, nu, xs, wg, wu, wd)


def _combine_kernel(dest_ref, x_ref, rw_ref, pe_ref, y_hbm, np_ref, wg_ref, wp_ref, fn_ref, o_ref, ybuf, sem,
                    *, final):
    tm = x_ref.shape[0]
    i = pl.program_id(0)
    n = pl.num_programs(0)

    def row_copy(slot, k, r, d):
        return pltpu.make_async_copy(y_hbm.at[pl.ds(d, 1)], ybuf.at[slot, k, pl.ds(r, 1)], sem.at[slot])

    def issue(tile, slot):
        def body(r, c):
            t = tile * tm + r
            row_copy(slot, 0, r, dest_ref[2 * t]).start()
            row_copy(slot, 1, r, dest_ref[2 * t + 1]).start()
            return c
        lax.fori_loop(0, tm, body, 0)

    slot = i & 1

    @pl.when(i == 0)
    def _():
        issue(0, 0)

    @pl.when(i + 1 < n)
    def _():
        issue(i + 1, 1 - slot)

    def drain(r, c):
        row_copy(slot, 0, 0, 0).wait()
        row_copy(slot, 1, 0, 0).wait()
        return c

    lax.fori_loop(0, tm, drain, 0)

    rw = rw_ref[...]
    x2 = x_ref[...] + rw[:, 0:1] * ybuf[slot, 0] + rw[:, 1:2] * ybuf[slot, 1]
    gate = _sigmoid(_dot(_rms(x2, np_ref[...]).astype(BF16), wg_ref[...]))
    x3 = x2 + gate * _dot(pe_ref[...].astype(BF16), wp_ref[...])
    o_ref[...] = _rms(x3, fn_ref[...]) if final else x3


def _combine(dest_flat, x1, rw, pemb, y, norm_ple, wg, wp, final_norm, final):
    t, d = x1.shape
    tm = TOK_TILE
    row = lambda n: pl.BlockSpec((tm, n), lambda i, dr: (i, 0))
    cfull = lambda shape: pl.BlockSpec(shape, lambda i, dr: (0,) * len(shape))
    grid_spec = pltpu.PrefetchScalarGridSpec(
        num_scalar_prefetch=1, grid=(t // tm,),
        in_specs=[row(d), row(LANES), row(PLE_DIM), pl.BlockSpec(memory_space=pl.ANY),
                  cfull((1, d)), cfull(wg.shape), cfull(wp.shape), cfull((1, d))],
        out_specs=row(d),
        scratch_shapes=[pltpu.VMEM((2, 2, tm, d), F32), pltpu.SemaphoreType.DMA((2,))])
    return pl.pallas_call(
        functools.partial(_combine_kernel, final=final), grid_spec=grid_spec,
        out_shape=jax.ShapeDtypeStruct((t, d), F32),
        compiler_params=_params(("arbitrary",)),
        name="moe_combine_ple",
    )(dest_flat, x1, rw, pemb, y, norm_ple.reshape(1, -1), wg, wp, final_norm.reshape(1, -1))


def _prompt_tables(rel_bias, s):
    tq = ATT_TILE
    n_cmp = (s - CMP_LEN) // CMP_STRIDE + 1
    n_sel = -(-s // SEL_LEN)
    head = np.arange(N_HEADS).reshape(N_KV, HPG)
    c = np.arange(LANES)
    bkt = _bucket_np(np.arange(s)[:, None] - (c * CMP_STRIDE + CMP_LEN - 1)[None])
    bias_cmp = rel_bias[bkt[None, None], head[:, :, None, None]]
    ql, kl = np.arange(tq)[:, None], np.arange(tq)[None]
    bk = np.stack([_bucket_np(ql - kl), _bucket_np(tq + ql - kl), _bucket_np(np.full((tq, tq), 2 * tq))])
    assert (_bucket_np(np.arange(tq + 1, 4 * tq)) == N_BUCKETS - 1).all()
    btile = rel_bias[bk[None, :, None], head[:, None, :, None, None]]
    btile = btile.reshape(N_KV, 3, HPG * tq, tq)
    cs, ss = np.arange(n_cmp) * CMP_STRIDE, np.arange(n_sel) * SEL_LEN
    overlap = ((cs[:, None] < ss[None] + SEL_LEN) & (cs[:, None] + CMP_LEN > ss[None])).astype(np.float32)
    ovt = np.zeros((LANES, LANES), np.float32)
    ovt[:n_sel, :n_cmp] = overlap.T
    emat = np.zeros((LANES, s), np.float32)
    emat[np.arange(s) // SEL_LEN, np.arange(s)] = 1.0
    return bias_cmp, btile, jnp.asarray(ovt, BF16), jnp.asarray(emat, BF16)


def _sample_tables(rel_bias, past, n_new, w_buf):
    cols = np.arange(LANES)
    head = cols // n_new
    qpos = past + cols % n_new
    n_pages = past // LANES
    n_cmp = (past + n_new - CMP_LEN) // CMP_STRIDE + 1
    n_sel = -(-(past + n_new) // SEL_LEN)
    assert n_cmp == past // CMP_STRIDE - 1 and HPG * n_new * N_KV == LANES
    c = np.arange(past // CMP_STRIDE)
    bct = rel_bias[_bucket_np(qpos[None] - (c * CMP_STRIDE + CMP_LEN - 1)[:, None]), head[None]]
    key = np.arange((n_pages + 1) * LANES)
    bst = rel_bias[_bucket_np(qpos[None] - key[:, None]), head[None]]
    kw = np.arange(w_buf + LANES)
    kpos = np.where(kw < w_buf, past - w_buf + kw, past + kw - w_buf)
    dw = qpos[None] - kpos[:, None]
    bwt = rel_bias[_bucket_np(dw), head[None]]
    mwt = ((dw >= 0) & (dw < WINDOW) & (kpos[:, None] >= 0) & (kw[:, None] < w_buf + n_new)).astype(np.float32)
    kn = np.arange(LANES)
    mnt = ((kn[:, None] < n_new) & (kn[:, None] <= (cols % n_new)[None])).astype(np.float32)
    cs, ss = np.arange(n_cmp) * CMP_STRIDE, np.arange(n_sel) * SEL_LEN
    overlap = ((cs[:, None] < ss[None] + SEL_LEN) & (cs[:, None] + CMP_LEN > ss[None])).astype(np.float32)
    rows_sel = -(-n_sel // 8) * 8
    ovt = np.zeros((rows_sel, past // CMP_STRIDE), np.float32)
    ovt[:n_sel, :n_cmp] = overlap.T
    same = (cols[:, None] // (HPG * n_new) == cols[None] // (HPG * n_new)) & \
           (cols[:, None] % n_new == cols[None] % n_new)
    return (bct, bst, bwt, jnp.asarray(mwt), jnp.asarray(mnt), jnp.asarray(ovt, BF16),
            jnp.asarray(same.astype(np.float32), BF16), n_sel)


def kernel(x_prompt, x_sample, cache_kv, state_win_kv, page_table, p_prompt, p_sample, rel_bias, norm_mix, w_in, v_norm, w_spatial, b_spatial, pe_cmp, w_phi1, w_phi2, w_branch, w_out, norm_ffn, w_router_group, b_router_group, w_router_expert, b_router_expert, w_exp_gate, w_exp_up, w_exp_down, norm_ple, w_ple_gate, w_ple_proj, final_norm):
    b, s, d = x_prompt.shape
    bs, n_new, _ = x_sample.shape
    depth = w_in.shape[0]
    n_pool, page = cache_kv.shape[1], cache_kv.shape[2]
    past = page_table.shape[1] * page
    w_buf = state_win_kv.shape[2]
    tp, ts = b * s, bs * n_new
    t = tp + ts
    assert page == LANES and tp % TOK_TILE == 0 and ts % TOK_TILE == 0 and n_new == 8 and s >= CHUNK

    x = jnp.concatenate([x_prompt.reshape(tp, d), x_sample.reshape(ts, d)], axis=0)
    pemb = jnp.concatenate([p_prompt.reshape(depth, tp, PLE_DIM), p_sample.reshape(depth, ts, PLE_DIM)], axis=1)
    cache4 = cache_kv.reshape(depth, n_pool, -1, LANES)
    swin = state_win_kv.reshape(depth, bs, w_buf, -1)
    page_flat = page_table.reshape(-1).astype(I32)

    bias_cmp, btile, ovt_p, emat = _prompt_tables(rel_bias, s)
    bct, bst, bwt, mwt, mnt, ovt_s, hsum, n_sel_s = _sample_tables(rel_bias, past, n_new, w_buf)

    ex = np.zeros((3, LANES, D_MODEL), np.float32)
    for br in range(3):
        ex[br, br * N_HEADS + np.arange(D_MODEL) // HEAD_DIM, np.arange(D_MODEL)] = 1.0
    ex = jnp.asarray(ex, BF16)
    tril = np.tril(np.ones((CHUNK, CHUNK), np.float32))
    blockdiag = np.kron(np.eye(CHUNK // n_new, dtype=np.float32), np.tril(np.ones((n_new, n_new), np.float32)))
    eye_g = jnp.eye(N_KV, dtype=BF16)
    eye_2 = jnp.eye(2, dtype=BF16)

    n_blocks = -(-2 * t // MOE_BLOCK) + N_EXPERTS
    cap = n_blocks * MOE_BLOCK

    kv_p, kv_s, win_p, v_s = [], [], [], []
    win_s = jnp.zeros(swin.shape, F32)
    for i in range(depth):
        gu, v, q, kvc, kvw, gn, gm = _inproj(x, norm_mix[i], w_in[i], v_norm[i])
        kv_p.append(kvc[:tp].reshape(b, s, 4, N_KV, HEAD_DIM))
        kv_s.append(kvc[tp:].reshape(bs, n_new, 4, N_KV, HEAD_DIM))
        win_p.append(kvw[:tp].reshape(b, s, 2, N_KV, HEAD_DIM)[:, s - min(WINDOW, s):])
        v_s.append(v[tp:].reshape(bs, n_new, A_WIDTH))

        q5 = q[:tp].reshape(b, s, N_KV, HPG, HEAD_DIM).transpose(0, 2, 3, 1, 4)
        kv4 = kvc[:tp].reshape(b, s, 4, N_KV, HEAD_DIM).transpose(2, 0, 3, 1, 4).astype(BF16)
        kvw2 = kvw[:tp].reshape(b, s, 2, N_KV, HEAD_DIM).transpose(2, 0, 3, 1, 4).astype(BF16)
        kh = kv4[:2].reshape(2, b, N_KV, s // CMP_STRIDE, CMP_STRIDE * HEAD_DIM)
        w1b = w_phi1[i].astype(BF16)
        w2b = w_phi2[i].astype(BF16)
        pe_flat = jnp.broadcast_to(pe_cmp[i].reshape(2, 1, CMP_LEN * HEAD_DIM), (2, 8, CMP_LEN * HEAD_DIM))
        kcv, pet = _cmp_prompt(kh, pe_flat.astype(BF16), w1b, w2b)
        o3_p = _nsa_prompt(q5, kcv, kv4, kvw2, bias_cmp, btile, ovt_p, emat)
        o3_p = o3_p.transpose(0, 1, 4, 2, 3, 5).reshape(3, tp, d)

        q_s = q[tp:].reshape(bs, n_new, N_KV, HPG, HEAD_DIM).transpose(0, 2, 3, 1, 4)
        qrows = (q_s[:, :, :, :, None, :] * eye_g[None, :, None, None, :, None]).reshape(bs, LANES, N_KV * HEAD_DIM)
        w1r = w1b.reshape(2, 2, CMP_STRIDE, HEAD_DIM, CMP_HIDDEN)
        wq = jnp.einsum('sprdn,ij->sridjpn', w1r, eye_2).reshape(2, CMP_STRIDE * LANES, 4 * CMP_HIDDEN)
        w2p = jnp.einsum('skd,gh->sgkhd', w2b, eye_g).reshape(2, N_KV, CMP_HIDDEN, N_KV * HEAD_DIM)
        o3_s, win_s = _nsa_sample(i, win_s, page_flat, cache4, qrows, kvc[tp:].reshape(bs, n_new, -1),
                                  kvw[tp:].reshape(bs, n_new, -1), swin, wq, pet, w2p,
                                  bct, bst, bwt, mwt, mnt, ovt_s, hsum, n_sel_s)
        o3_s = o3_s.reshape(3, bs, N_KV, HPG, n_new, HEAD_DIM).transpose(0, 1, 4, 2, 3, 5).reshape(3, ts, d)
        o3 = jnp.concatenate([o3_p, o3_s], axis=1)

        ws = w_spatial[i]
        wmix = jnp.stack([ws * tril, jnp.tile(ws[:, :n_new, :n_new], (1, CHUNK // n_new, CHUNK // n_new)) * blockdiag])
        bsp = b_spatial[i]
        bmix = jnp.stack([jnp.repeat(bsp.T, A_GROUP_WIDTH, axis=1),
                          jnp.repeat(jnp.tile(bsp[:, :n_new], (1, CHUNK // n_new)).T, A_GROUP_WIDTH, axis=1)])
        wr = jnp.concatenate([w_router_group[i], w_router_expert[i]], axis=1)
        wr = jnp.pad(wr, ((0, 0), (0, LANES - wr.shape[1])))
        wrh = wr.astype(BF16)
        wrl = (wr - wrh.astype(F32)).astype(BF16)
        brr = jnp.pad(jnp.concatenate([b_router_group[i], b_router_expert[i]]), (0, LANES - N_GROUPS - N_EXPERTS))
        x1, h2, ei, rw = _merge(x, gu, v, o3, gn, gm, wmix.astype(BF16), bmix, ex, w_branch[i].astype(BF16),
                                w_out[i].astype(BF16), norm_ffn[i], wrh, wrl, brr.reshape(1, LANES),
                                tp // TOK_TILE)

        dest_flat, be, nu = _route(ei, n_blocks)
        xs = _dispatch(dest_flat, h2, cap)
        y = _experts(be, nu, xs, w_exp_gate[i].astype(BF16), w_exp_up[i].astype(BF16),
                     w_exp_down[i].astype(BF16))
        x = _combine(dest_flat, x1, rw, pemb[i], y, norm_ple[i], w_ple_gate[i].astype(BF16),
                     w_ple_proj[i].astype(BF16), final_norm, final=(i == depth - 1))

    y_prompt = x[:tp].reshape(b, s, d)
    y_sample = x[tp:].reshape(bs, n_new, d)
    return (y_prompt, y_sample, jnp.stack(kv_p), jnp.stack(kv_s), jnp.stack(win_p),
            win_s.reshape(depth, bs, w_buf, 2, N_KV, HEAD_DIM), jnp.stack(v_s))
```

```python
import functools
import math

import numpy as np
import jax
import jax.numpy as jnp
from jax import lax
from jax.experimental import pallas as pl
from jax.experimental.pallas import tpu as pltpu

F32 = jnp.float32
BF16 = jnp.bfloat16
I32 = jnp.int32

D_MODEL = 1024
A_WIDTH = 1024
A_GROUPS = 4
A_GROUP_WIDTH = A_WIDTH // A_GROUPS
CHUNK = 128
N_HEADS = 16
HEAD_DIM = 64
N_KV = 4
HPG = N_HEADS // N_KV
CMP_LEN = 32
CMP_STRIDE = 16
CMP_HIDDEN = 256
SEL_LEN = 64
SEL_TOP = 16
WINDOW = 512
N_BUCKETS = 32
MAX_DISTANCE = 128
N_GROUPS = 4
EXPERTS_PER_GROUP = 8
N_EXPERTS = 32
D_EXPERT = 512
PLE_DIM = 256
EPS = 1e-6
NEG = -1e30
FORCE = 1e9
NEG_PAD = -3e38

LANES = 128
TOK_TILE = 256
ATT_TILE = 128
MOE_BLOCK = 256
BIAS_ZERO_ROW = WINDOW + ATT_TILE
VMEM_LIMIT = 56 * 1024 * 1024


def _dot(a, b):
    return jnp.dot(a, b, preferred_element_type=F32)


def _dot_nt(a, b):
    return lax.dot_general(a, b, (((1,), (1,)), ((), ())), preferred_element_type=F32)


def _dot_tn(a, b):
    return lax.dot_general(a, b, (((0,), (0,)), ((), ())), preferred_element_type=F32)


def _hilo(a):
    hi = a.astype(BF16)
    lo = (a - hi.astype(F32)).astype(BF16)
    return hi, lo


def _dot_hilo_l(a, b):
    hi, lo = _hilo(a)
    return _dot(hi, b) + _dot(lo, b)


def _gelu(x):
    return 0.5 * x * (1.0 + jnp.tanh(0.7978845608028654 * (x + 0.044715 * (x * x * x))))


def _sigmoid(x):
    return 1.0 / (1.0 + jnp.exp(-x))


def _rms(x, gain):
    return x * lax.rsqrt(jnp.mean(x * x, axis=-1, keepdims=True) + EPS) * gain


def _full(shape):
    nd = len(shape)
    return pl.BlockSpec(shape, lambda *_: (0,) * nd)


def _params(sem, vmem=VMEM_LIMIT):
    return pltpu.CompilerParams(dimension_semantics=sem, vmem_limit_bytes=vmem)


def _bucket_np(dist):
    n = np.maximum(np.asarray(dist, np.int64), 0)
    max_exact = N_BUCKETS // 2
    nf = np.maximum(n, max_exact).astype(np.float64)
    large = max_exact + (np.log(nf / max_exact) / math.log(MAX_DISTANCE / max_exact)
                         * (N_BUCKETS - max_exact)).astype(np.int64)
    return np.where(n < max_exact, n, np.minimum(large, N_BUCKETS - 1)).astype(np.int32)


def _inproj_kernel(x_ref, g_ref, wu, wv, wq, wkc, wkw, wgn, wgm, vn_ref,
                   gu_o, v_o, q_o, kvc_o, kvw_o, gn_o, gm_o):
    x = x_ref[...]
    hb = _rms(x, g_ref[...]).astype(BF16)
    gu_o[...] = _gelu(_dot(hb, wu[...])).astype(BF16)
    v_o[...] = _rms(_gelu(_dot(hb, wv[...])), vn_ref[...])
    q_o[...] = (_dot(hb, wq[...]) * (HEAD_DIM ** -0.5)).astype(BF16)
    kvc_o[...] = _dot(hb, wkc[...])
    kvw_o[...] = _dot(hb, wkw[...])
    gn_o[...] = _sigmoid(_dot(hb, wgn[...]))
    gm_o[...] = _sigmoid(_dot(hb, wgm[...])).astype(BF16)


def _inproj(x, gain, w_in, v_gain):
    t = x.shape[0]
    tm = TOK_TILE
    a = A_WIDTH
    c_q, c_kv, c_gn, c_gm = 2 * a, 3 * a, 3 * a + 1536, 3 * a + 1536 + 48
    wb = w_in.astype(BF16)
    wu, wv, wq = wb[:, :a], wb[:, a:2 * a], wb[:, c_q:c_kv]
    wkc, wkw = wb[:, c_kv:c_kv + 1024], wb[:, c_kv + 1024:c_gn]
    wgn = jnp.pad(wb[:, c_gn:c_gm], ((0, 0), (0, LANES - 48)))
    wgm = wb[:, c_gm:]
    row = lambda n: pl.BlockSpec((tm, n), lambda i: (i, 0))
    outs = [(a, BF16), (a, F32), (a, BF16), (1024, F32), (512, F32), (LANES, F32), (2 * D_MODEL, BF16)]
    return pl.pallas_call(
        _inproj_kernel,
        grid=(t // tm,),
        in_specs=[row(D_MODEL), _full((1, D_MODEL)), _full(wu.shape), _full(wv.shape), _full(wq.shape),
                  _full(wkc.shape), _full(wkw.shape), _full(wgn.shape), _full(wgm.shape), _full((1, a))],
        out_specs=[row(n) for n, _ in outs],
        out_shape=[jax.ShapeDtypeStruct((t, n), d) for n, d in outs],
        compiler_params=_params(("parallel",)),
        name="inproj",
    )(x, gain.reshape(1, -1), wu, wv, wq, wkc, wkw, wgn, wgm, v_gain.reshape(1, -1))


def _half_block_products(tap, wq_ref, slot, nrow):
    acc = jnp.zeros((nrow, 4 * CMP_HIDDEN), F32)
    for r2 in range(CMP_STRIDE // 2):
        lhs = jnp.concatenate([tap(2 * r2), tap(2 * r2 + 1)], axis=1).astype(BF16)
        acc = acc + _dot(lhs, wq_ref[slot, r2 * 2 * LANES:(r2 + 1) * 2 * LANES, :])
    return acc


def _block_summaries(acc, gi, pe_row, w2, nrow):
    c0 = gi * 2 * CMP_HIDDEN
    pre = acc[:, c0:c0 + CMP_HIDDEN] + pltpu.roll(acc[:, c0 + CMP_HIDDEN:c0 + 2 * CMP_HIDDEN], nrow - 1, 0) + pe_row
    return _dot(_gelu(pre).astype(BF16), w2)


def _cmp_prompt_kernel(kv_ref, pe_ref, w1_ref, wq_ref, w2_ref, o_ref, pt_ref):
    nrow = kv_ref.shape[0] // CMP_STRIDE
    pe_term = _dot(pe_ref[0], w1_ref[0])
    acc = _half_block_products(lambda r: kv_ref[pl.ds(r, nrow, stride=CMP_STRIDE), :], wq_ref, 0, nrow)
    out = jnp.zeros((nrow, LANES), F32)
    for gi in range(2):
        out = out + _block_summaries(acc, gi, pe_term[0:1], w2_ref[0, gi], nrow)
    o_ref[0, 0, 0] = out.astype(BF16)
    pt_ref[0] = pe_term


def _cmp_prompt(kvc, b, s, pe_flat, w1, wq, w2pair):
    nrow = s // CMP_STRIDE
    return pl.pallas_call(
        _cmp_prompt_kernel,
        grid=(2, b, 2),
        in_specs=[pl.BlockSpec((s, LANES), lambda sl, i, p: (i, 2 * sl + p)),
                  pl.BlockSpec((1,) + pe_flat.shape[1:], lambda sl, i, p: (sl, 0, 0)),
                  pl.BlockSpec((1,) + w1.shape[1:], lambda sl, i, p: (sl, 0, 0)),
                  pl.BlockSpec((1,) + wq.shape[1:], lambda sl, i, p: (sl, 0, 0)),
                  pl.BlockSpec((1,) + w2pair.shape[1:], lambda sl, i, p: (sl, 0, 0, 0))],
        out_specs=[pl.BlockSpec((1, 1, 1, nrow, LANES), lambda sl, i, p: (sl, i, p, 0, 0)),
                   pl.BlockSpec((1, 8, CMP_HIDDEN), lambda sl, i, p: (sl, 0, 0))],
        out_shape=[jax.ShapeDtypeStruct((2, b, 2, nrow, LANES), BF16),
                   jax.ShapeDtypeStruct((2, 8, CMP_HIDDEN), F32)],
        compiler_params=_params(("arbitrary", "arbitrary", "arbitrary")),
        name="cmp_prompt",
    )(kvc, pe_flat, w1, wq, w2pair)


def _top_blocks(score, n_sel, n_top):
    row = lax.broadcasted_iota(I32, score.shape, 0)
    cnt = jnp.zeros(score.shape, F32)
    for j in range(n_sel):
        sj = score[j:j + 1, :]
        beats = jnp.where(sj > score, 1.0, jnp.where(sj == score, jnp.where(row > j, 1.0, 0.0), 0.0))
        cnt = cnt + beats
    return jnp.where((cnt < n_top) & (row < n_sel), 1.0, 0.0)


def _nsa_prompt_kernel(q_ref, kc_ref, vc_ref, ks_ref, vs_ref, kw_ref, vw_ref, bc_ref, bz_ref, ovt_ref, sp_ref,
                       pc_ref, o_ref, sel_ref, m_ref, l_ref, acc_ref, *, n_cmp, n_sel, n_top):
    tq = ATT_TILE
    tk = ATT_TILE
    cols = HPG * tq
    i = pl.program_id(2)
    q = jnp.concatenate([_dot(q_ref[...], sp_ref[0, h]).astype(BF16) for h in range(HPG)], axis=0)

    def to_tokens(o_t):
        out = jnp.zeros((tq, HPG * HEAD_DIM), F32)
        for h in range(HPG):
            out = out + _dot_tn(o_t[:, h * tq:(h + 1) * tq].astype(BF16), pc_ref[0, h])
        return out.astype(BF16)

    s = _dot_nt(kc_ref[0, 0, 0], q) + bc_ref[0, 0]
    c_io = lax.broadcasted_iota(I32, (LANES, cols), 0)
    l_io = lax.broadcasted_iota(I32, (LANES, cols), 1)
    qpos = i * tq + (l_io & (tq - 1))
    valid = (qpos >= c_io * CMP_STRIDE + (CMP_LEN - 1)) & (c_io < n_cmp)
    s = jnp.where(valid, s, NEG)
    e = jnp.exp(s - jnp.max(s, axis=0, keepdims=True))
    p = jnp.where(valid, e / jnp.sum(e, axis=0, keepdims=True), 0.0)
    o_ref[0] = to_tokens(_dot_tn(vc_ref[0, 0, 0], p.astype(BF16)))

    psum = p[:, 0:tq] + p[:, tq:2 * tq] + p[:, 2 * tq:3 * tq] + p[:, 3 * tq:4 * tq]
    p_hi, p_lo = _hilo(psum)
    imp = _dot(ovt_ref[...], p_hi) + _dot(ovt_ref[...], p_lo)
    j_io = lax.broadcasted_iota(I32, (LANES, tq), 0)
    qp2 = i * tq + lax.broadcasted_iota(I32, (LANES, tq), 1)
    qblk = lax.shift_right_logical(qp2, int(math.log2(SEL_LEN)))
    forced = (j_io == 0) | (j_io == qblk) | (j_io == qblk - 1)
    score = jnp.where(forced, FORCE, jnp.where(j_io <= qblk, imp, NEG))
    score = jnp.where(j_io < n_sel, score, NEG_PAD)
    sel_ref[...] = _top_blocks(score, n_sel, n_top)

    def band_scores(k_ref, k0, nk, off):
        k = k_ref[pl.ds(pl.multiple_of(k0, tq), nk), :].astype(BF16)
        u0 = pl.multiple_of(jnp.maximum(BIAS_ZERO_ROW - off, 0), tq)
        s = _dot_nt(k, q) + bz_ref[0, pl.ds(u0, nk), :]
        d = (off + (lax.broadcasted_iota(I32, (nk, cols), 1) & (tq - 1))) - lax.broadcasted_iota(I32, (nk, cols), 0)
        return s, d

    tkc = 4 * tk
    per_chunk = tkc // SEL_LEN

    def slc_chunk(c):
        k0 = c * tkc
        s, d = band_scores(ks_ref, k0, tkc, i * tq - k0)
        rows = [jnp.broadcast_to(sel_ref[pl.ds(c * per_chunk + t, 1), :], (SEL_LEN, tq)) for t in range(per_chunk)]
        picked = jnp.concatenate([jnp.concatenate(rows, axis=0)] * HPG, axis=1) > 0.5
        s = jnp.where(picked & (d >= 0), s, NEG)
        return s, vs_ref[pl.ds(pl.multiple_of(k0, tq), tkc), :].astype(BF16)

    c_diag = lax.shift_right_logical(i, 2)
    s, v = slc_chunk(c_diag)
    m = jnp.max(s, axis=0, keepdims=True)
    pp = jnp.exp(s - m)
    m_ref[...] = jnp.broadcast_to(m, m_ref.shape)
    l_ref[...] = jnp.broadcast_to(jnp.sum(pp, axis=0, keepdims=True), l_ref.shape)
    acc_ref[...] = _dot_tn(v, pp.astype(BF16))

    def earlier_chunk(c, carry):
        s, v = slc_chunk(c)
        m_prev = m_ref[0:1, :]
        m_new = jnp.maximum(m_prev, jnp.max(s, axis=0, keepdims=True))
        alpha = jnp.exp(m_prev - m_new)
        pp = jnp.exp(s - m_new)
        l_ref[...] = jnp.broadcast_to(alpha * l_ref[0:1, :] + jnp.sum(pp, axis=0, keepdims=True), l_ref.shape)
        acc_ref[...] = alpha * acc_ref[...] + _dot_tn(v, pp.astype(BF16))
        m_ref[...] = jnp.broadcast_to(m_new, m_ref.shape)
        return carry

    lax.fori_loop(0, c_diag, earlier_chunk, 0)
    o_ref[1] = to_tokens(acc_ref[...] / l_ref[0:1, :])

    k0 = jnp.maximum(i * tq - WINDOW, 0)
    s, d = band_scores(kw_ref, k0, WINDOW + tq, i * tq - k0)
    s = jnp.where((d >= 0) & (d < WINDOW), s, NEG)
    pp = jnp.exp(s - jnp.max(s, axis=0, keepdims=True))
    vw = vw_ref[pl.ds(pl.multiple_of(k0, tq), WINDOW + tq), :].astype(BF16)
    o_ref[2] = to_tokens(_dot_tn(vw, pp.astype(BF16)) / jnp.sum(pp, axis=0, keepdims=True))


def _nsa_prompt(q, kcv, kvc, kvw, b, bias_cmp, btile, ovt):
    s = kcv.shape[3] * CMP_STRIDE
    tq = ATT_TILE
    nq = s // tq
    n_cmp = (s - CMP_LEN) // CMP_STRIDE + 1
    n_sel = -(-s // SEL_LEN)
    n_top = min(SEL_TOP, n_sel)
    assert kcv.shape[3] == LANES and s % tq == 0 and WINDOW % tq == 0
    spread = np.zeros((2, HPG, HPG * HEAD_DIM, LANES), np.float32)
    dd = np.arange(HEAD_DIM)
    for gi in range(2):
        for h in range(HPG):
            spread[gi, h, h * HEAD_DIM + dd, gi * HEAD_DIM + dd] = 1.0
    collect = jnp.asarray(spread.transpose(0, 1, 3, 2), BF16)
    spread = jnp.asarray(spread, BF16)
    cols = HPG * tq
    kv_lane = lambda blk: pl.BlockSpec((s, LANES), lambda bi, gi, i: (bi, blk + gi // 2))
    cspec = lambda slot: pl.BlockSpec((1, 1, 1, LANES, LANES), lambda bi, gi, i: (slot, bi, gi // 2, 0, 0))
    return pl.pallas_call(
        functools.partial(_nsa_prompt_kernel, n_cmp=n_cmp, n_sel=n_sel, n_top=n_top),
        grid=(b, N_KV, nq),
        in_specs=[pl.BlockSpec((tq, HPG * HEAD_DIM), lambda bi, gi, i: (bi * nq + i, gi)),
                  cspec(0), cspec(1), kv_lane(4), kv_lane(6), kv_lane(0), kv_lane(2),
                  pl.BlockSpec((1, 1, LANES, cols), lambda bi, gi, i: (gi, i, 0, 0)),
                  pl.BlockSpec((1,) + btile.shape[1:], lambda bi, gi, i: (gi, 0, 0)),
                  _full(ovt.shape),
                  pl.BlockSpec((1,) + spread.shape[1:], lambda bi, gi, i: (gi % 2, 0, 0, 0)),
                  pl.BlockSpec((1,) + collect.shape[1:], lambda bi, gi, i: (gi % 2, 0, 0, 0))],
        out_specs=pl.BlockSpec((3, tq, HPG * HEAD_DIM), lambda bi, gi, i: (0, bi * nq + i, gi)),
        out_shape=jax.ShapeDtypeStruct((3, b * s, N_HEADS * HEAD_DIM), BF16),
        scratch_shapes=[pltpu.VMEM((LANES, tq), F32), pltpu.VMEM((8, cols), F32), pltpu.VMEM((8, cols), F32),
                        pltpu.VMEM((LANES, cols), F32)],
        compiler_params=_params(("parallel", "parallel", "arbitrary")),
        name="nsa_prompt",
    )(q, kcv, kcv, kvc, kvc, kvw, kvw, bias_cmp, btile, ovt, spread, collect)


def _nsa_sample_kernel(pt_ref, *refs, n_pages, n_sel, n_top):
    pages = refs[:n_pages]
    (q_ref, kvcn_ref, kvwn_ref, swin_ref, wq_ref, pet_ref, w2p_ref, bct_ref, bst_ref, bwt_ref, mwt_ref, mnt_ref,
     ovt_ref, hsum_ref, hm_ref, fold_ref, foldt_ref, selq_ref, o_ref, win_ref, s_ref) = refs[n_pages:]
    del pt_ref
    gd = N_KV * HEAD_DIM
    nrow = n_pages * (LANES // CMP_STRIDE)
    n_chunk = pages[0].shape[2] // LANES
    q_rep = jnp.concatenate([q_ref[0].astype(F32)] * N_HEADS, axis=0) * hm_ref[...]
    qr = _dot(q_rep.astype(BF16), fold_ref[...]).astype(BF16)

    def to_tokens(o):
        x = _dot(o.astype(BF16), foldt_ref[...]) * hm_ref[...]
        return _dot(selq_ref[...], x.astype(BF16)).astype(BF16)

    def page_cols(pg, slot):
        return jnp.concatenate([pg[0, 0, pl.ds(2 * slot + h, LANES, stride=n_chunk), :] for h in range(2)], axis=1)

    kc_all = []
    for slot in range(2):
        out = jnp.zeros((nrow, gd), F32)
        for pair in range(2):
            chunk = 2 * slot + pair

            def tap(r, chunk=chunk):
                return jnp.concatenate(
                    [pg[0, 0, pl.ds(r * n_chunk + chunk, LANES // CMP_STRIDE, stride=CMP_STRIDE * n_chunk), :]
                     for pg in pages], axis=0)

            acc = _half_block_products(tap, wq_ref, slot, nrow)
            for gi in range(2):
                out = out + _block_summaries(acc, gi, pet_ref[slot, 0:1], w2p_ref[slot, 2 * pair + gi], nrow)
        kc_all.append(out.astype(BF16))

    cols = LANES

    def softmax_t(s):
        m = jnp.max(s, axis=0, keepdims=True)
        e = jnp.exp(s - m)
        return e / jnp.sum(e, axis=0, keepdims=True)

    def as_column(row):
        return jnp.broadcast_to(row, (cols, cols)).T[:, 0:1]

    n_cmp = nrow - 1
    c_io = lax.broadcasted_iota(I32, (nrow, cols), 0)
    s = _dot_nt(kc_all[0], qr) + bct_ref[...]
    valid = c_io < n_cmp
    p = jnp.where(valid, softmax_t(jnp.where(valid, s, NEG)), 0.0)
    o_ref[0, 0] = to_tokens(_dot_tn(p.astype(BF16), kc_all[1]))

    pg_sum = _dot_hilo_l(p, hsum_ref[...])
    g_hi, g_lo = _hilo(pg_sum)
    imp = _dot(ovt_ref[...], g_hi) + _dot(ovt_ref[...], g_lo)
    j_io = lax.broadcasted_iota(I32, imp.shape, 0)
    qblk = n_sel - 1
    forced = (j_io == 0) | (j_io == qblk) | (j_io == qblk - 1)
    score = jnp.where(forced, FORCE, jnp.where(j_io <= qblk, imp, NEG))
    score = jnp.where(j_io < n_sel, score, NEG_PAD)
    sel = _top_blocks(score, n_sel, n_top)

    per_page = LANES // SEL_LEN
    for k, pg in enumerate(pages):
        kp = page_cols(pg, 2).astype(BF16)
        sk = _dot_nt(kp, qr) + bst_ref[k * LANES:(k + 1) * LANES, :]
        mk = jnp.concatenate([jnp.broadcast_to(sel[per_page * k + t:per_page * k + t + 1, :], (SEL_LEN, cols))
                              for t in range(per_page)], axis=0)
        s_ref[k * LANES:(k + 1) * LANES, :] = jnp.where(mk > 0.5, sk, NEG)
    zpad = jnp.zeros((LANES - 8, gd), F32)
    kn = jnp.concatenate([kvcn_ref[0][:, 2 * gd:3 * gd], zpad], axis=0).astype(BF16)
    vn = jnp.concatenate([kvcn_ref[0][:, 3 * gd:4 * gd], zpad], axis=0).astype(BF16)
    sn = _dot_nt(kn, qr) + bst_ref[n_pages * LANES:(n_pages + 1) * LANES, :]
    mn = (mnt_ref[...] > 0.5) & (jnp.broadcast_to(sel[n_sel - 1:n_sel, :], (LANES, cols)) > 0.5)
    s_ref[n_pages * LANES:(n_pages + 1) * LANES, :] = jnp.where(mn, sn, NEG)
    n_keys = (n_pages + 1) * LANES
    m = jnp.max(s_ref[0:n_keys, :], axis=0, keepdims=True)
    den = jnp.zeros((1, cols), F32)
    o = jnp.zeros((cols, gd), F32)
    for k in range(n_pages + 1):
        pk = jnp.exp(s_ref[k * LANES:(k + 1) * LANES, :] - m)
        den = den + jnp.sum(pk, axis=0, keepdims=True)
        vk = vn if k == n_pages else page_cols(pages[k], 3).astype(BF16)
        o = o + _dot_tn(pk.astype(BF16), vk)
    o_ref[1, 0] = to_tokens(o / as_column(den))

    w_buf = swin_ref.shape[2]
    kw = swin_ref[0, 0][:, 0:gd].astype(BF16)
    vw = swin_ref[0, 0][:, gd:2 * gd].astype(BF16)
    kwn = jnp.concatenate([kvwn_ref[0][:, 0:gd], zpad], axis=0).astype(BF16)
    vwn = jnp.concatenate([kvwn_ref[0][:, gd:2 * gd], zpad], axis=0).astype(BF16)
    sw = jnp.concatenate([_dot_nt(kw, qr), _dot_nt(kwn, qr)], axis=0) + bwt_ref[...]
    sw = jnp.where(mwt_ref[...] > 0.5, sw, NEG)
    mw = jnp.max(sw, axis=0, keepdims=True)
    pw = jnp.exp(sw - mw)
    denw = jnp.sum(pw, axis=0, keepdims=True)
    ow = _dot_tn(pw[0:w_buf].astype(BF16), vw) + _dot_tn(pw[w_buf:].astype(BF16), vwn)
    o_ref[2, 0] = to_tokens(ow / as_column(denw))

    n_new = kvwn_ref.shape[1]
    win_ref[0, 0, 0:w_buf - n_new, :] = swin_ref[0, 0, n_new:w_buf, :]
    win_ref[0, 0, w_buf - n_new:w_buf, :] = kvwn_ref[0]


def _nsa_sample(layer, win_prev, page_flat, cache4, q_s, kvcn, kvwn, swin, wq, pet, w2p,
                bct, bst, bwt, mwt, mnt, ovt, hsum, n_sel):
    bs, n_new, d = q_s.shape
    depth = swin.shape[0]
    r_head = np.arange(LANES) // n_new
    c_head = np.arange(d) // HEAD_DIM
    hm = jnp.asarray((r_head[:, None] == c_head[None]).astype(np.float32))
    fold_np = np.zeros((d, N_KV * HEAD_DIM), np.float32)
    fold_np[np.arange(d), (c_head // HPG) * HEAD_DIM + np.arange(d) % HEAD_DIM] = 1.0
    fold, foldt = jnp.asarray(fold_np, BF16), jnp.asarray(fold_np.T, BF16)
    selq = jnp.asarray((np.arange(n_new)[:, None] == (np.arange(LANES) % n_new)[None]).astype(np.float32), BF16)
    n_pages = page_flat.shape[0] // bs
    w_buf = swin.shape[2]
    n_top = min(SEL_TOP, n_sel)
    page_specs = [pl.BlockSpec((1, 1, cache4.shape[2], LANES),
                               functools.partial(lambda b, pt, k: (layer, pt[b * n_pages + k], 0, 0), k=k))
                  for k in range(n_pages)]
    cfull = lambda a: pl.BlockSpec(a.shape, lambda b, pt: (0,) * a.ndim)
    in_specs = page_specs + [
        pl.BlockSpec((1, n_new, d), lambda b, pt: (b, 0, 0)),
        pl.BlockSpec((1,) + kvcn.shape[1:], lambda b, pt: (b, 0, 0)),
        pl.BlockSpec((1,) + kvwn.shape[1:], lambda b, pt: (b, 0, 0)),
        pl.BlockSpec((1, 1, w_buf, swin.shape[3]), lambda b, pt: (layer, b, 0, 0)),
        cfull(wq), cfull(pet), cfull(w2p), cfull(bct), cfull(bst), cfull(bwt), cfull(mwt), cfull(mnt),
        cfull(ovt), cfull(hsum), cfull(hm), cfull(fold), cfull(foldt), cfull(selq)]
    args = [cache4] * n_pages + [q_s, kvcn, kvwn, swin, wq, pet, w2p, bct, bst, bwt, mwt, mnt, ovt, hsum,
                                 hm, fold, foldt, selq]
    in_specs.append(pl.BlockSpec(memory_space=pl.ANY))
    args.append(win_prev)
    aliases = {len(args): 1}

    def body(*refs):
        refs = refs[:len(in_specs)] + refs[len(in_specs) + 1:]
        _nsa_sample_kernel(*refs, n_pages=n_pages, n_sel=n_sel, n_top=n_top)

    grid_spec = pltpu.PrefetchScalarGridSpec(
        num_scalar_prefetch=1, grid=(bs,), in_specs=in_specs,
        out_specs=[pl.BlockSpec((3, 1, n_new, d), lambda b, pt: (0, b, 0, 0)),
                   pl.BlockSpec((1, 1, w_buf, swin.shape[3]), lambda b, pt: (layer, b, 0, 0))],
        scratch_shapes=[pltpu.VMEM(((n_pages + 1) * LANES, LANES), F32)])
    return pl.pallas_call(
        body, grid_spec=grid_spec,
        out_shape=[jax.ShapeDtypeStruct((3, bs, n_new, d), BF16),
                   jax.ShapeDtypeStruct((depth, bs, w_buf, swin.shape[3]), F32)],
        input_output_aliases=aliases,
        compiler_params=_params(("arbitrary",)),
        name="nsa_sample",
    )(page_flat, *args)


def _merge_kernel(x_ref, gu_ref, v_ref, o3p_ref, o3s_ref, gn_ref, gm_ref, wmix_ref, bmix_ref, ex_ref, wb_ref,
                  wo_ref, nf_ref, wrh_ref, wrl_ref, br_ref, x1_o, h2_o, ei_o, rw_o, *, n_prompt_tiles):
    tm = x_ref.shape[0]
    is_prompt = pl.program_id(0) < n_prompt_tiles
    mixed = []
    for c in range(tm // CHUNK):
        vb = v_ref[c * CHUNK:(c + 1) * CHUNK, :].astype(BF16)
        mixed.append(jnp.concatenate(
            [_dot(wmix_ref[0, g], vb[:, g * A_GROUP_WIDTH:(g + 1) * A_GROUP_WIDTH]) for g in range(A_GROUPS)],
            axis=1) + bmix_ref[0])
    o_a = gu_ref[...].astype(F32) * jnp.concatenate(mixed, axis=0)
    gn = gn_ref[...]
    o_b = jnp.zeros((tm, D_MODEL), F32)
    for br in range(3):
        o_br = jnp.where(is_prompt, o3p_ref[br], o3s_ref[br])
        o_b = o_b + _dot_hilo_l(gn, ex_ref[br]) * o_br.astype(F32)
    gm = gm_ref[...].astype(F32)
    merged = (gm[:, :D_MODEL] * _dot(o_a.astype(BF16), wb_ref[0])
              + gm[:, D_MODEL:] * _dot(o_b.astype(BF16), wb_ref[1]))
    x1 = x_ref[...] + _dot(merged.astype(BF16), wo_ref[...])
    x1_o[...] = x1
    h2 = _rms(x1, nf_ref[...])
    h2_o[...] = h2
    hh, hl = _hilo(h2)
    logit = _dot(hh, wrh_ref[...]) + _dot(hl, wrh_ref[...]) + _dot(hh, wrl_ref[...]) + br_ref[...]
    lane = lax.broadcasted_iota(I32, logit.shape, 1)
    big = jnp.int32(9999)
    is_g = lane < N_GROUPS
    gl = jnp.where(is_g, logit, -jnp.inf)
    gmax = jnp.max(gl, axis=-1, keepdims=True)
    grp = jnp.min(jnp.where(gl == gmax, lane, big), axis=-1, keepdims=True)
    p_grp = 1.0 / jnp.sum(jnp.where(is_g, jnp.exp(logit - gmax), 0.0), axis=-1, keepdims=True)
    e_lane = lane - N_GROUPS
    in_grp = (e_lane >= 0) & (lax.shift_right_arithmetic(e_lane, 3) == grp) & (e_lane < N_EXPERTS)
    el = jnp.where(in_grp, logit, -jnp.inf)
    t1 = jnp.max(el, axis=-1, keepdims=True)
    i1 = jnp.min(jnp.where(el == t1, lane, big), axis=-1, keepdims=True)
    el2 = jnp.where(lane == i1, -jnp.inf, el)
    t2 = jnp.max(el2, axis=-1, keepdims=True)
    i2 = jnp.min(jnp.where(el2 == t2, lane, big), axis=-1, keepdims=True)
    r = jnp.exp(t2 - t1)
    w1 = p_grp / (1.0 + r)
    w2 = p_grp * r / (1.0 + r)
    ei_o[...] = jnp.where(lane == 0, i1 - N_GROUPS, jnp.where(lane == 1, i2 - N_GROUPS, 0))
    rw_o[...] = jnp.where(lane == 0, w1, jnp.where(lane == 1, w2, 0.0))


def _merge(x, gu, v, o3p, o3s, gn, gm, wmix, bmix, ex, wb, wo, nf, wrh, wrl, brr, n_prompt_tiles):
    t = x.shape[0]
    tm = TOK_TILE
    row = lambda n: pl.BlockSpec((tm, n), lambda i: (i, 0))
    kind = lambda i: jnp.where(i < n_prompt_tiles, 0, 1)
    outs = [(D_MODEL, F32), (D_MODEL, F32), (LANES, I32), (LANES, F32)]
    return pl.pallas_call(
        functools.partial(_merge_kernel, n_prompt_tiles=n_prompt_tiles),
        grid=(t // tm,),
        in_specs=[row(D_MODEL), row(A_WIDTH), row(A_WIDTH),
                  pl.BlockSpec((3, tm, D_MODEL), lambda i: (0, jnp.minimum(i, n_prompt_tiles - 1), 0)),
                  pl.BlockSpec((3, tm, D_MODEL), lambda i: (0, jnp.maximum(i - n_prompt_tiles, 0), 0)),
                  row(LANES), row(2 * D_MODEL),
                  pl.BlockSpec((1, A_GROUPS, CHUNK, CHUNK), lambda i: (kind(i), 0, 0, 0)),
                  pl.BlockSpec((1, CHUNK, A_WIDTH), lambda i: (kind(i), 0, 0)),
                  _full(ex.shape), _full(wb.shape), _full(wo.shape), _full((1, D_MODEL)),
                  _full(wrh.shape), _full(wrl.shape), _full((1, LANES))],
        out_specs=[row(n) for n, _ in outs],
        out_shape=[jax.ShapeDtypeStruct((t, n), d) for n, d in outs],
        compiler_params=_params(("parallel",)),
        name="merge",
    )(x, gu, v, o3p, o3s, gn, gm, wmix, bmix, ex, wb, wo, nf.reshape(1, -1), wrh, wrl, brr)


def _route1_kernel(ei_ref, ltri_ref, rank_o, cnt_o, carry):
    @pl.when(pl.program_id(0) == 0)
    def _():
        carry[...] = jnp.zeros(carry.shape, F32)

    ei = ei_ref[...]
    lane = lax.broadcasted_iota(I32, ei.shape, 1)
    e1, e2 = ei[:, 0:1], ei[:, 1:2]
    oh = jnp.where((lane == e1) | (lane == e2), 1.0, 0.0)
    cum = _dot(ltri_ref[...], oh.astype(BF16)) + carry[0:1, :]
    r1 = jnp.sum(jnp.where(lane == e1, cum, 0.0), axis=-1, keepdims=True)
    r2 = jnp.sum(jnp.where(lane == e2, cum, 0.0), axis=-1, keepdims=True)
    rank_o[...] = jnp.where(lane == 0, r1, jnp.where(lane == 1, r2, 0.0)).astype(I32)
    carry[...] = carry[...] + jnp.sum(oh, axis=0, keepdims=True)
    cnt_o[...] = carry[...]


def _route2_kernel(ei_ref, rank_ref, cnt_ref, utri_ref, dest_o, be_o, nu_o):
    shift = int(math.log2(MOE_BLOCK))
    nb = lax.shift_right_logical(cnt_ref[...].astype(I32) + (MOE_BLOCK - 1), shift).astype(F32)
    start = _dot(nb.astype(BF16), utri_ref[...])
    ei = ei_ref[...]
    lane = lax.broadcasted_iota(I32, ei.shape, 1)
    e1, e2 = ei[:, 0:1], ei[:, 1:2]
    s1 = jnp.sum(jnp.where(lane == e1, start[0:1, :], 0.0), axis=-1, keepdims=True)
    s2 = jnp.sum(jnp.where(lane == e2, start[0:1, :], 0.0), axis=-1, keepdims=True)
    rk = rank_ref[...]
    d1 = s1.astype(I32) * MOE_BLOCK + rk[:, 0:1]
    d2 = s2.astype(I32) * MOE_BLOCK + rk[:, 1:2]
    dest_o[...] = jnp.where(lane == 0, d1, jnp.where(lane == 1, d2, 0))
    end = start[0:1, :] + nb[0:1, :]
    j = lax.broadcasted_iota(I32, be_o.shape, 0).astype(F32)
    l2 = lax.broadcasted_iota(I32, be_o.shape, 1)
    ge = jnp.sum(jnp.where((l2 < N_EXPERTS) & (end <= j), 1.0, 0.0), axis=-1, keepdims=True)
    be_o[...] = jnp.broadcast_to(jnp.minimum(ge, N_EXPERTS - 1.0).astype(I32), be_o.shape)
    l3 = lax.broadcasted_iota(I32, nu_o.shape, 1)
    nu = jnp.sum(jnp.where(l3 == N_EXPERTS - 1, jnp.broadcast_to(end, nu_o.shape), 0.0), axis=-1, keepdims=True)
    nu_o[...] = jnp.broadcast_to(nu.astype(I32), nu_o.shape)


def _route(ei, n_blocks):
    t = ei.shape[0]
    tm = TOK_TILE
    row = pl.BlockSpec((tm, LANES), lambda i: (i, 0))
    ltri = jnp.asarray(np.tril(np.ones((tm, tm), np.float32), -1), BF16)
    utri = jnp.asarray(np.triu(np.ones((LANES, LANES), np.float32), 1), BF16)
    rank, cnt = pl.pallas_call(
        _route1_kernel, grid=(t // tm,),
        in_specs=[row, _full((tm, tm))],
        out_specs=[row, _full((8, LANES))],
        out_shape=[jax.ShapeDtypeStruct((t, LANES), I32), jax.ShapeDtypeStruct((8, LANES), F32)],
        scratch_shapes=[pltpu.VMEM((8, LANES), F32)],
        compiler_params=_params(("arbitrary",)),
        name="route_rank",
    )(ei, ltri)
    nbp = -(-n_blocks // 8) * 8
    dest, be, nu = pl.pallas_call(
        _route2_kernel, grid=(t // tm,),
        in_specs=[row, row, _full((8, LANES)), _full((LANES, LANES))],
        out_specs=[row, _full((nbp, LANES)), _full((8, LANES))],
        out_shape=[jax.ShapeDtypeStruct((t, LANES), I32), jax.ShapeDtypeStruct((nbp, LANES), I32),
                   jax.ShapeDtypeStruct((8, LANES), I32)],
        compiler_params=_params(("arbitrary",)),
        name="route_slots",
    )(ei, rank, cnt, utri)
    return dest[:, :2].reshape(-1), be[:n_blocks, 0], nu[0, :1]


def _dispatch_kernel(dest_ref, h_ref, xs_in, xs_out, sem, *, chunk):
    del xs_in
    i = pl.program_id(0)

    def row_copy(r, d):
        return pltpu.make_async_copy(h_ref.at[pl.ds(r, 1)], xs_out.at[pl.ds(d, 1)], sem)

    def issue(r, c):
        t = i * chunk + r
        row_copy(r, dest_ref[2 * t]).start()
        row_copy(r, dest_ref[2 * t + 1]).start()
        return c

    def drain(r, c):
        row_copy(0, 0).wait()
        row_copy(0, 0).wait()
        return c

    lax.fori_loop(0, chunk, issue, 0, unroll=8)
    lax.fori_loop(0, chunk, drain, 0, unroll=8)


def _dispatch(dest_flat, h2, cap):
    t, d = h2.shape
    chunk = TOK_TILE
    grid_spec = pltpu.PrefetchScalarGridSpec(
        num_scalar_prefetch=1, grid=(t // chunk,),
        in_specs=[pl.BlockSpec((chunk, d), lambda i, dr: (i, 0)), pl.BlockSpec(memory_space=pl.ANY)],
        out_specs=pl.BlockSpec(memory_space=pl.ANY),
        scratch_shapes=[pltpu.SemaphoreType.DMA(())])
    return pl.pallas_call(
        functools.partial(_dispatch_kernel, chunk=chunk), grid_spec=grid_spec,
        out_shape=jax.ShapeDtypeStruct((cap, d), h2.dtype),
        input_output_aliases={2: 0},
        compiler_params=_params(("arbitrary",)),
        name="moe_dispatch",
    )(dest_flat, h2, jnp.zeros((cap, d), h2.dtype))


def _expert_kernel(be_ref, nu_ref, x_ref, wg_ref, wu_ref, wd_ref, y_ref):
    i = pl.program_id(0)

    @pl.when(i < nu_ref[0])
    def _():
        xb = x_ref[...].astype(BF16)
        g = _dot(xb, wg_ref[0])
        u = _dot(xb, wu_ref[0])
        a = (g * _sigmoid(g) * u).astype(BF16)
        y_ref[...] = _dot(a, wd_ref[0])

    @pl.when(i >= nu_ref[0])
    def _():
        y_ref[...] = jnp.zeros(y_ref.shape, F32)


def _experts(be, nu, xs, wg, wu, wd):
    cap, d = xs.shape
    bm = MOE_BLOCK
    xin = lambda i, be, nu: (jnp.minimum(i, jnp.maximum(nu[0] - 1, 0)), 0)
    wsel = lambda i, be, nu: (be[i], 0, 0)
    grid_spec = pltpu.PrefetchScalarGridSpec(
        num_scalar_prefetch=2, grid=(cap // bm,),
        in_specs=[pl.BlockSpec((bm, d), xin),
                  pl.BlockSpec((1, d, D_EXPERT), wsel), pl.BlockSpec((1, d, D_EXPERT), wsel),
                  pl.BlockSpec((1, D_EXPERT, d), wsel)],
        out_specs=pl.BlockSpec((bm, d), lambda i, be, nu: (i, 0)))
    return pl.pallas_call(
        _expert_kernel, grid_spec=grid_spec,
        out_shape=jax.ShapeDtypeStruct((cap, d), F32),
        compiler_params=_params(("arbitrary",)),
        name="moe_experts",
    )(be, nu, xs, wg, wu, wd)


def _combine_kernel(dest_ref, x_ref, rw_ref, pe_ref, y_hbm, np_ref, wg_ref, wp_ref, fn_ref, o_ref, ybuf, sem,
                    *, final):
    tm = x_ref.shape[0]
    i = pl.program_id(0)
    n = pl.num_programs(0)

    def row_copy(slot, k, r, d):
        return pltpu.make_async_copy(y_hbm.at[pl.ds(d, 1)], ybuf.at[slot, k, pl.ds(r, 1)], sem.at[slot])

    def issue(tile, slot):
        def body(r, c):
            t = tile * tm + r
            row_copy(slot, 0, r, dest_ref[2 * t]).start()
            row_copy(slot, 1, r, dest_ref[2 * t + 1]).start()
            return c
        lax.fori_loop(0, tm, body, 0)

    slot = i & 1

    @pl.when(i == 0)
    def _():
        issue(0, 0)

    @pl.when(i + 1 < n)
    def _():
        issue(i + 1, 1 - slot)

    def drain(r, c):
        row_copy(slot, 0, 0, 0).wait()
        row_copy(slot, 1, 0, 0).wait()
        return c

    lax.fori_loop(0, tm, drain, 0)

    rw = rw_ref[...]
    x2 = x_ref[...] + rw[:, 0:1] * ybuf[slot, 0] + rw[:, 1:2] * ybuf[slot, 1]
    gate = _sigmoid(_dot(_rms(x2, np_ref[...]).astype(BF16), wg_ref[...]))
    x3 = x2 + gate * _dot(pe_ref[...].astype(BF16), wp_ref[...])
    o_ref[...] = _rms(x3, fn_ref[...]) if final else x3


def _combine(dest_flat, x1, rw, pemb, y, norm_ple, wg, wp, final_norm, final):
    t, d = x1.shape
    tm = TOK_TILE
    row = lambda n: pl.BlockSpec((tm, n), lambda i, dr: (i, 0))
    cfull = lambda shape: pl.BlockSpec(shape, lambda i, dr: (0,) * len(shape))
    grid_spec = pltpu.PrefetchScalarGridSpec(
        num_scalar_prefetch=1, grid=(t // tm,),
        in_specs=[row(d), row(LANES), row(PLE_DIM), pl.BlockSpec(memory_space=pl.ANY),
                  cfull((1, d)), cfull(wg.shape), cfull(wp.shape), cfull((1, d))],
        out_specs=row(d),
        scratch_shapes=[pltpu.VMEM((2, 2, tm, d), F32), pltpu.SemaphoreType.DMA((2,))])
    return pl.pallas_call(
        functools.partial(_combine_kernel, final=final), grid_spec=grid_spec,
        out_shape=jax.ShapeDtypeStruct((t, d), F32),
        compiler_params=_params(("arbitrary",)),
        name="moe_combine_ple",
    )(dest_flat, x1, rw, pemb, y, norm_ple.reshape(1, -1), wg, wp, final_norm.reshape(1, -1))


def _per_head_lookup(rel_bias, bkt):
    onehot = (jnp.arange(N_BUCKETS, dtype=I32)[:, None] == jnp.asarray(bkt.reshape(1, -1), I32)).astype(F32)
    tab = jnp.dot(rel_bias.T, onehot, precision=lax.Precision.HIGHEST)
    return tab.reshape((N_HEADS,) + bkt.shape)


def _prompt_tables(rel_bias, s):
    tq = ATT_TILE
    nq = s // tq
    n_cmp = (s - CMP_LEN) // CMP_STRIDE + 1
    n_sel = -(-s // SEL_LEN)
    c = np.arange(LANES)
    qpos = np.arange(s).reshape(nq, 1, tq)
    bkt = _bucket_np(qpos - (c * CMP_STRIDE + CMP_LEN - 1)[None, :, None])
    bias_cmp = _per_head_lookup(rel_bias, bkt).reshape(N_KV, HPG, nq, LANES, tq)
    bias_cmp = bias_cmp.transpose(0, 2, 3, 1, 4).reshape(N_KV, nq, LANES, HPG * tq)
    u, ql = np.arange(2 * BIAS_ZERO_ROW)[:, None], np.arange(tq)[None]
    assert (_bucket_np(np.arange(BIAS_ZERO_ROW - 4 * tq + 1, 2 * s)) == N_BUCKETS - 1).all()
    btile = _per_head_lookup(rel_bias, _bucket_np(ql - u + BIAS_ZERO_ROW)).reshape(N_KV, HPG, 2 * BIAS_ZERO_ROW, tq)
    btile = btile.transpose(0, 2, 1, 3).reshape(N_KV, 2 * BIAS_ZERO_ROW, HPG * tq)
    cs, ss = np.arange(n_cmp) * CMP_STRIDE, np.arange(n_sel) * SEL_LEN
    overlap = ((cs[:, None] < ss[None] + SEL_LEN) & (cs[:, None] + CMP_LEN > ss[None])).astype(np.float32)
    ovt = np.zeros((LANES, LANES), np.float32)
    ovt[:n_sel, :n_cmp] = overlap.T
    return bias_cmp, btile, jnp.asarray(ovt, BF16)


def _per_column_lookup(rel_bias, bkt, n_new):
    rows = bkt.shape[0]
    onehot = (jnp.asarray(np.repeat(bkt, N_BUCKETS, axis=1), I32)
              == jnp.asarray(np.tile(np.arange(N_BUCKETS), n_new)[None], I32)).astype(F32)
    spread = (rel_bias[None, :, :, None] * jnp.eye(n_new, dtype=F32)[:, None, None, :])
    spread = spread.reshape(n_new * N_BUCKETS, N_HEADS * n_new)
    return jnp.dot(onehot, spread, precision=lax.Precision.HIGHEST).reshape(rows, N_HEADS * n_new)


def _sample_tables(rel_bias, past, n_new, w_buf):
    cols = np.arange(LANES)
    qpos = past + cols % n_new
    qnew = past + np.arange(n_new)
    n_pages = past // LANES
    n_cmp = (past + n_new - CMP_LEN) // CMP_STRIDE + 1
    n_sel = -(-(past + n_new) // SEL_LEN)
    assert n_cmp == past // CMP_STRIDE - 1 and HPG * n_new * N_KV == LANES
    c = np.arange(past // CMP_STRIDE)
    bct = _per_column_lookup(rel_bias, _bucket_np(qnew[None] - (c * CMP_STRIDE + CMP_LEN - 1)[:, None]), n_new)
    key = np.arange((n_pages + 1) * LANES)
    bst = _per_column_lookup(rel_bias, _bucket_np(qnew[None] - key[:, None]), n_new)
    kw = np.arange(w_buf + LANES)
    kpos = np.where(kw < w_buf, past - w_buf + kw, past + kw - w_buf)
    bwt = _per_column_lookup(rel_bias, _bucket_np(qnew[None] - kpos[:, None]), n_new)
    dw = qpos[None] - kpos[:, None]
    mwt = ((dw >= 0) & (dw < WINDOW) & (kpos[:, None] >= 0) & (kw[:, None] < w_buf + n_new)).astype(np.float32)
    kn = np.arange(LANES)
    mnt = ((kn[:, None] < n_new) & (kn[:, None] <= (cols % n_new)[None])).astype(np.float32)
    cs, ss = np.arange(n_cmp) * CMP_STRIDE, np.arange(n_sel) * SEL_LEN
    overlap = ((cs[:, None] < ss[None] + SEL_LEN) & (cs[:, None] + CMP_LEN > ss[None])).astype(np.float32)
    rows_sel = -(-n_sel // 8) * 8
    ovt = np.zeros((rows_sel, past // CMP_STRIDE), np.float32)
    ovt[:n_sel, :n_cmp] = overlap.T
    same = (cols[:, None] // (HPG * n_new) == cols[None] // (HPG * n_new)) & \
           (cols[:, None] % n_new == cols[None] % n_new)
    return (bct, bst, bwt, jnp.asarray(mwt), jnp.asarray(mnt), jnp.asarray(ovt, BF16),
            jnp.asarray(same.astype(np.float32), BF16), n_sel)


def kernel(x_prompt, x_sample, cache_kv, state_win_kv, page_table, p_prompt, p_sample, rel_bias, norm_mix, w_in, v_norm, w_spatial, b_spatial, pe_cmp, w_phi1, w_phi2, w_branch, w_out, norm_ffn, w_router_group, b_router_group, w_router_expert, b_router_expert, w_exp_gate, w_exp_up, w_exp_down, norm_ple, w_ple_gate, w_ple_proj, final_norm):
    b, s, d = x_prompt.shape
    bs, n_new, _ = x_sample.shape
    depth = w_in.shape[0]
    n_pool, page = cache_kv.shape[1], cache_kv.shape[2]
    past = page_table.shape[1] * page
    w_buf = state_win_kv.shape[2]
    tp, ts = b * s, bs * n_new
    t = tp + ts
    assert page == LANES and tp % TOK_TILE == 0 and ts % TOK_TILE == 0 and n_new == 8 and s >= CHUNK

    x = jnp.concatenate([x_prompt.reshape(tp, d), x_sample.reshape(ts, d)], axis=0)
    pemb = jnp.concatenate([p_prompt.reshape(depth, tp, PLE_DIM), p_sample.reshape(depth, ts, PLE_DIM)], axis=1)
    cache4 = cache_kv.reshape(depth, n_pool, -1, LANES)
    swin = state_win_kv.reshape(depth, bs, w_buf, -1)
    page_flat = page_table.reshape(-1).astype(I32)

    bias_cmp, btile, ovt_p = _prompt_tables(rel_bias, s)
    bct, bst, bwt, mwt, mnt, ovt_s, hsum, n_sel_s = _sample_tables(rel_bias, past, n_new, w_buf)

    ex = np.zeros((3, LANES, D_MODEL), np.float32)
    for br in range(3):
        ex[br, br * N_HEADS + np.arange(D_MODEL) // HEAD_DIM, np.arange(D_MODEL)] = 1.0
    ex = jnp.asarray(ex, BF16)
    tril = np.tril(np.ones((CHUNK, CHUNK), np.float32))
    blockdiag = np.kron(np.eye(CHUNK // n_new, dtype=np.float32), np.tril(np.ones((n_new, n_new), np.float32)))
    eye_g = jnp.eye(N_KV, dtype=BF16)
    eye_2 = jnp.eye(2, dtype=BF16)

    n_blocks = -(-2 * t // MOE_BLOCK) + N_EXPERTS
    cap = n_blocks * MOE_BLOCK

    kv_p, kv_s, win_p, v_s = [], [], [], []
    win_s = jnp.zeros(swin.shape, F32)
    for i in range(depth):
        gu, v, q, kvc, kvw, gn, gm = _inproj(x, norm_mix[i], w_in[i], v_norm[i])
        kv_p.append(kvc[:tp].reshape(b, s, 4, N_KV, HEAD_DIM))
        kv_s.append(kvc[tp:].reshape(bs, n_new, 4, N_KV, HEAD_DIM))
        win_p.append(kvw[:tp].reshape(b, s, 2, N_KV, HEAD_DIM)[:, s - min(WINDOW, s):])
        v_s.append(v[tp:].reshape(bs, n_new, A_WIDTH))

        w1b = w_phi1[i].astype(BF16)
        w2b = w_phi2[i].astype(BF16)
        pe_flat = jnp.broadcast_to(pe_cmp[i].reshape(2, 1, CMP_LEN * HEAD_DIM), (2, 8, CMP_LEN * HEAD_DIM))
        w1r = w1b.reshape(2, 2, CMP_STRIDE, HEAD_DIM, CMP_HIDDEN)
        wq = jnp.einsum('sprdn,ij->sridjpn', w1r, eye_2).reshape(2, CMP_STRIDE * LANES, 4 * CMP_HIDDEN)
        w2p = jnp.einsum('skd,gh->sgkhd', w2b, eye_g).reshape(2, N_KV, CMP_HIDDEN, N_KV * HEAD_DIM)
        w2pair = jnp.einsum('skd,gh->sgkhd', w2b, eye_2).reshape(2, 2, CMP_HIDDEN, LANES)

        kcv, pet = _cmp_prompt(kvc, b, s, pe_flat.astype(BF16), w1b, wq, w2pair)
        o3_p = _nsa_prompt(q, kcv, kvc, kvw, b, bias_cmp, btile, ovt_p)

        o3_s, win_s = _nsa_sample(i, win_s, page_flat, cache4, q[tp:].reshape(bs, n_new, d),
                                  kvc[tp:].reshape(bs, n_new, -1), kvw[tp:].reshape(bs, n_new, -1), swin,
                                  wq, pet, w2p, bct, bst, bwt, mwt, mnt, ovt_s, hsum, n_sel_s)
        o3_s = o3_s.reshape(3, ts, d)

        ws = w_spatial[i]
        wmix = jnp.stack([ws * tril, jnp.tile(ws[:, :n_new, :n_new], (1, CHUNK // n_new, CHUNK // n_new)) * blockdiag])
        bsp = b_spatial[i]
        bmix = jnp.stack([jnp.repeat(bsp.T, A_GROUP_WIDTH, axis=1),
                          jnp.repeat(jnp.tile(bsp[:, :n_new], (1, CHUNK // n_new)).T, A_GROUP_WIDTH, axis=1)])
        wr = jnp.concatenate([w_router_group[i], w_router_expert[i]], axis=1)
        wr = jnp.pad(wr, ((0, 0), (0, LANES - wr.shape[1])))
        wrh = wr.astype(BF16)
        wrl = (wr - wrh.astype(F32)).astype(BF16)
        brr = jnp.pad(jnp.concatenate([b_router_group[i], b_router_expert[i]]), (0, LANES - N_GROUPS - N_EXPERTS))
        x1, h2, ei, rw = _merge(x, gu, v, o3_p, o3_s, gn, gm, wmix.astype(BF16), bmix, ex, w_branch[i].astype(BF16),
                                w_out[i].astype(BF16), norm_ffn[i], wrh, wrl, brr.reshape(1, LANES),
                                tp // TOK_TILE)

        dest_flat, be, nu = _route(ei, n_blocks)
        xs = _dispatch(dest_flat, h2, cap)
        y = _experts(be, nu, xs, w_exp_gate[i].astype(BF16), w_exp_up[i].astype(BF16),
                     w_exp_down[i].astype(BF16))
        x = _combine(dest_flat, x1, rw, pemb[i], y, norm_ple[i], w_ple_gate[i].astype(BF16),
                     w_ple_proj[i].astype(BF16), final_norm, final=(i == depth - 1))

    y_prompt = x[:tp].reshape(b, s, d)
    y_sample = x[tp:].reshape(bs, n_new, d)
    return (y_prompt, y_sample, jnp.stack(kv_p), jnp.stack(kv_s), jnp.stack(win_p),
            win_s.reshape(depth, bs, w_buf, 2, N_KV, HEAD_DIM), jnp.stack(v_s))
```

```python
import functools
import math

import numpy as np
import jax
import jax.numpy as jnp
from jax import lax
from jax.experimental import pallas as pl
from jax.experimental.pallas import tpu as pltpu

F32 = jnp.float32
BF16 = jnp.bfloat16
I32 = jnp.int32

D_MODEL = 1024
A_WIDTH = 1024
A_GROUPS = 4
A_GROUP_WIDTH = A_WIDTH // A_GROUPS
CHUNK = 128
N_HEADS = 16
HEAD_DIM = 64
N_KV = 4
HPG = N_HEADS // N_KV
CMP_LEN = 32
CMP_STRIDE = 16
CMP_HIDDEN = 256
SEL_LEN = 64
SEL_TOP = 16
WINDOW = 512
N_BUCKETS = 32
MAX_DISTANCE = 128
N_GROUPS = 4
EXPERTS_PER_GROUP = 8
N_EXPERTS = 32
D_EXPERT = 512
PLE_DIM = 256
EPS = 1e-6
NEG = -1e30
FORCE = 1e9
NEG_PAD = -3e38

LANES = 128
TOK_TILE = 256
ATT_TILE = 128
MOE_BLOCK = 256
BIAS_ZERO_ROW = WINDOW + ATT_TILE
VMEM_LIMIT = 56 * 1024 * 1024


def _dot(a, b):
    return jnp.dot(a, b, preferred_element_type=F32)


def _dot_nt(a, b):
    return lax.dot_general(a, b, (((1,), (1,)), ((), ())), preferred_element_type=F32)


def _dot_tn(a, b):
    return lax.dot_general(a, b, (((0,), (0,)), ((), ())), preferred_element_type=F32)


def _hilo(a):
    hi = a.astype(BF16)
    lo = (a - hi.astype(F32)).astype(BF16)
    return hi, lo


def _dot_hilo_l(a, b):
    hi, lo = _hilo(a)
    return _dot(hi, b) + _dot(lo, b)


def _gelu(x):
    return 0.5 * x * (1.0 + jnp.tanh(0.7978845608028654 * (x + 0.044715 * (x * x * x))))


def _sigmoid(x):
    return 1.0 / (1.0 + jnp.exp(-x))


def _rms(x, gain):
    return x * lax.rsqrt(jnp.mean(x * x, axis=-1, keepdims=True) + EPS) * gain


def _full(shape):
    nd = len(shape)
    return pl.BlockSpec(shape, lambda *_: (0,) * nd)


def _params(sem, vmem=VMEM_LIMIT):
    return pltpu.CompilerParams(dimension_semantics=sem, vmem_limit_bytes=vmem)


def _bucket_np(dist):
    n = np.maximum(np.asarray(dist, np.int64), 0)
    max_exact = N_BUCKETS // 2
    nf = np.maximum(n, max_exact).astype(np.float64)
    large = max_exact + (np.log(nf / max_exact) / math.log(MAX_DISTANCE / max_exact)
                         * (N_BUCKETS - max_exact)).astype(np.int64)
    return np.where(n < max_exact, n, np.minimum(large, N_BUCKETS - 1)).astype(np.int32)


def _inproj_kernel(x_ref, g_ref, wu, wv, wq, wkc, wkw, wgn, wgm, vn_ref,
                   gu_o, v_o, q_o, kvc_o, kvw_o, gn_o, gm_o):
    x = x_ref[...]
    hb = _rms(x, g_ref[...]).astype(BF16)
    gu_o[...] = _gelu(_dot(hb, wu[...])).astype(BF16)
    v_o[...] = _rms(_gelu(_dot(hb, wv[...])), vn_ref[...])
    q_o[...] = (_dot(hb, wq[...]) * (HEAD_DIM ** -0.5)).astype(BF16)
    kvc_o[...] = _dot(hb, wkc[...])
    kvw_o[...] = _dot(hb, wkw[...])
    gn_o[...] = _sigmoid(_dot(hb, wgn[...]))
    gm_o[...] = _sigmoid(_dot(hb, wgm[...])).astype(BF16)


def _inproj(x, gain, w_in, v_gain):
    t = x.shape[0]
    tm = TOK_TILE
    a = A_WIDTH
    c_q, c_kv, c_gn, c_gm = 2 * a, 3 * a, 3 * a + 1536, 3 * a + 1536 + 48
    wb = w_in.astype(BF16)
    wu, wv, wq = wb[:, :a], wb[:, a:2 * a], wb[:, c_q:c_kv]
    wkc, wkw = wb[:, c_kv:c_kv + 1024], wb[:, c_kv + 1024:c_gn]
    wgn = jnp.pad(wb[:, c_gn:c_gm], ((0, 0), (0, LANES - 48)))
    wgm = wb[:, c_gm:]
    row = lambda n: pl.BlockSpec((tm, n), lambda i: (i, 0))
    outs = [(a, BF16), (a, F32), (a, BF16), (1024, F32), (512, F32), (LANES, F32), (2 * D_MODEL, BF16)]
    return pl.pallas_call(
        _inproj_kernel,
        grid=(t // tm,),
        in_specs=[row(D_MODEL), _full((1, D_MODEL)), _full(wu.shape), _full(wv.shape), _full(wq.shape),
                  _full(wkc.shape), _full(wkw.shape), _full(wgn.shape), _full(wgm.shape), _full((1, a))],
        out_specs=[row(n) for n, _ in outs],
        out_shape=[jax.ShapeDtypeStruct((t, n), d) for n, d in outs],
        compiler_params=_params(("parallel",)),
        name="inproj",
    )(x, gain.reshape(1, -1), wu, wv, wq, wkc, wkw, wgn, wgm, v_gain.reshape(1, -1))


def _half_block_products(tap, wq_ref, slot, nrow):
    acc = jnp.zeros((nrow, 4 * CMP_HIDDEN), F32)
    for r2 in range(CMP_STRIDE // 2):
        lhs = jnp.concatenate([tap(2 * r2), tap(2 * r2 + 1)], axis=1).astype(BF16)
        acc = acc + _dot(lhs, wq_ref[slot, r2 * 2 * LANES:(r2 + 1) * 2 * LANES, :])
    return acc


def _block_summaries(acc, gi, pe_row, w2, nrow):
    c0 = gi * 2 * CMP_HIDDEN
    pre = acc[:, c0:c0 + CMP_HIDDEN] + pltpu.roll(acc[:, c0 + CMP_HIDDEN:c0 + 2 * CMP_HIDDEN], nrow - 1, 0) + pe_row
    return _dot(_gelu(pre).astype(BF16), w2)


def _cmp_prompt_kernel(kv_ref, pe_ref, w1_ref, wq_ref, w2_ref, o_ref, pt_ref):
    nrow = kv_ref.shape[0] // CMP_STRIDE
    pe_term = _dot(pe_ref[0], w1_ref[0])
    acc = _half_block_products(lambda r: kv_ref[pl.ds(r, nrow, stride=CMP_STRIDE), :], wq_ref, 0, nrow)
    out = jnp.zeros((nrow, LANES), F32)
    for gi in range(2):
        out = out + _block_summaries(acc, gi, pe_term[0:1], w2_ref[0, gi], nrow)
    o_ref[0, 0, 0] = out.astype(BF16)
    pt_ref[0] = pe_term


def _cmp_prompt(kvc, b, s, pe_flat, w1, wq, w2pair):
    nrow = s // CMP_STRIDE
    return pl.pallas_call(
        _cmp_prompt_kernel,
        grid=(2, b, 2),
        in_specs=[pl.BlockSpec((s, LANES), lambda sl, i, p: (i, 2 * sl + p)),
                  pl.BlockSpec((1,) + pe_flat.shape[1:], lambda sl, i, p: (sl, 0, 0)),
                  pl.BlockSpec((1,) + w1.shape[1:], lambda sl, i, p: (sl, 0, 0)),
                  pl.BlockSpec((1,) + wq.shape[1:], lambda sl, i, p: (sl, 0, 0)),
                  pl.BlockSpec((1,) + w2pair.shape[1:], lambda sl, i, p: (sl, 0, 0, 0))],
        out_specs=[pl.BlockSpec((1, 1, 1, nrow, LANES), lambda sl, i, p: (sl, i, p, 0, 0)),
                   pl.BlockSpec((1, 8, CMP_HIDDEN), lambda sl, i, p: (sl, 0, 0))],
        out_shape=[jax.ShapeDtypeStruct((2, b, 2, nrow, LANES), BF16),
                   jax.ShapeDtypeStruct((2, 8, CMP_HIDDEN), F32)],
        compiler_params=_params(("arbitrary", "arbitrary", "arbitrary")),
        name="cmp_prompt",
    )(kvc, pe_flat, w1, wq, w2pair)


def _top_blocks(score, n_sel, n_top):
    row = lax.broadcasted_iota(I32, score.shape, 0)
    cnt = jnp.zeros(score.shape, F32)
    for j in range(n_sel):
        sj = score[j:j + 1, :]
        beats = jnp.where(sj > score, 1.0, jnp.where(sj == score, jnp.where(row > j, 1.0, 0.0), 0.0))
        cnt = cnt + beats
    return jnp.where((cnt < n_top) & (row < n_sel), 1.0, 0.0)


def _nsa_prompt_kernel(q_ref, kc_ref, vc_ref, ks_ref, vs_ref, kw_ref, vw_ref, bc_ref, bz_ref, ovt_ref, sp_ref,
                       pc_ref, o_ref, sel_ref, m_ref, l_ref, acc_ref, *, n_cmp, n_sel, n_top):
    tq = ATT_TILE
    tk = ATT_TILE
    cols = HPG * tq
    i = pl.program_id(2)
    q = jnp.concatenate([_dot(q_ref[...], sp_ref[0, h]).astype(BF16) for h in range(HPG)], axis=0)

    def to_tokens(o_t):
        out = jnp.zeros((tq, HPG * HEAD_DIM), F32)
        for h in range(HPG):
            out = out + _dot_tn(o_t[:, h * tq:(h + 1) * tq].astype(BF16), pc_ref[0, h])
        return out.astype(BF16)

    s = _dot_nt(kc_ref[0, 0, 0], q) + bc_ref[0, 0]
    c_io = lax.broadcasted_iota(I32, (LANES, cols), 0)
    l_io = lax.broadcasted_iota(I32, (LANES, cols), 1)
    qpos = i * tq + (l_io & (tq - 1))
    valid = (qpos >= c_io * CMP_STRIDE + (CMP_LEN - 1)) & (c_io < n_cmp)
    s = jnp.where(valid, s, NEG)
    e = jnp.exp(s - jnp.max(s, axis=0, keepdims=True))
    p = jnp.where(valid, e / jnp.sum(e, axis=0, keepdims=True), 0.0)
    o_ref[0] = to_tokens(_dot_tn(vc_ref[0, 0, 0], p.astype(BF16)))

    psum = p[:, 0:tq] + p[:, tq:2 * tq] + p[:, 2 * tq:3 * tq] + p[:, 3 * tq:4 * tq]
    p_hi, p_lo = _hilo(psum)
    imp = _dot(ovt_ref[...], p_hi) + _dot(ovt_ref[...], p_lo)
    j_io = lax.broadcasted_iota(I32, (LANES, tq), 0)
    qp2 = i * tq + lax.broadcasted_iota(I32, (LANES, tq), 1)
    qblk = lax.shift_right_logical(qp2, int(math.log2(SEL_LEN)))
    forced = (j_io == 0) | (j_io == qblk) | (j_io == qblk - 1)
    score = jnp.where(forced, FORCE, jnp.where(j_io <= qblk, imp, NEG))
    score = jnp.where(j_io < n_sel, score, NEG_PAD)
    sel_ref[...] = _top_blocks(score, n_sel, n_top)

    def band_scores(k_ref, k0, nk, off):
        k = k_ref[pl.ds(pl.multiple_of(k0, tq), nk), :].astype(BF16)
        u0 = pl.multiple_of(jnp.maximum(BIAS_ZERO_ROW - off, 0), tq)
        s = _dot_nt(k, q) + bz_ref[0, pl.ds(u0, nk), :]
        d = (off + (lax.broadcasted_iota(I32, (nk, cols), 1) & (tq - 1))) - lax.broadcasted_iota(I32, (nk, cols), 0)
        return s, d

    tkc = 4 * tk
    per_chunk = tkc // SEL_LEN

    def slc_chunk(c):
        k0 = c * tkc
        s, d = band_scores(ks_ref, k0, tkc, i * tq - k0)
        rows = [jnp.broadcast_to(sel_ref[pl.ds(c * per_chunk + t, 1), :], (SEL_LEN, tq)) for t in range(per_chunk)]
        picked = jnp.concatenate([jnp.concatenate(rows, axis=0)] * HPG, axis=1) > 0.5
        s = jnp.where(picked & (d >= 0), s, NEG)
        return s, vs_ref[pl.ds(pl.multiple_of(k0, tq), tkc), :].astype(BF16)

    c_diag = lax.shift_right_logical(i, 2)
    s, v = slc_chunk(c_diag)
    m = jnp.max(s, axis=0, keepdims=True)
    pp = jnp.exp(s - m)
    m_ref[...] = jnp.broadcast_to(m, m_ref.shape)
    l_ref[...] = jnp.broadcast_to(jnp.sum(pp, axis=0, keepdims=True), l_ref.shape)
    acc_ref[...] = _dot_tn(v, pp.astype(BF16))

    def earlier_chunk(c, carry):
        s, v = slc_chunk(c)
        m_prev = m_ref[0:1, :]
        m_new = jnp.maximum(m_prev, jnp.max(s, axis=0, keepdims=True))
        alpha = jnp.exp(m_prev - m_new)
        pp = jnp.exp(s - m_new)
        l_ref[...] = jnp.broadcast_to(alpha * l_ref[0:1, :] + jnp.sum(pp, axis=0, keepdims=True), l_ref.shape)
        acc_ref[...] = alpha * acc_ref[...] + _dot_tn(v, pp.astype(BF16))
        m_ref[...] = jnp.broadcast_to(m_new, m_ref.shape)
        return carry

    lax.fori_loop(0, c_diag, earlier_chunk, 0)
    o_ref[1] = to_tokens(acc_ref[...] / l_ref[0:1, :])

    k0 = jnp.maximum(i * tq - WINDOW, 0)
    s, d = band_scores(kw_ref, k0, WINDOW + tq, i * tq - k0)
    s = jnp.where((d >= 0) & (d < WINDOW), s, NEG)
    pp = jnp.exp(s - jnp.max(s, axis=0, keepdims=True))
    vw = vw_ref[pl.ds(pl.multiple_of(k0, tq), WINDOW + tq), :].astype(BF16)
    o_ref[2] = to_tokens(_dot_tn(vw, pp.astype(BF16)) / jnp.sum(pp, axis=0, keepdims=True))


def _nsa_prompt(q, kcv, kvc, kvw, b, bias_cmp, btile, ovt):
    s = kcv.shape[3] * CMP_STRIDE
    tq = ATT_TILE
    nq = s // tq
    n_cmp = (s - CMP_LEN) // CMP_STRIDE + 1
    n_sel = -(-s // SEL_LEN)
    n_top = min(SEL_TOP, n_sel)
    assert kcv.shape[3] == LANES and s % tq == 0 and WINDOW % tq == 0
    spread = np.zeros((2, HPG, HPG * HEAD_DIM, LANES), np.float32)
    dd = np.arange(HEAD_DIM)
    for gi in range(2):
        for h in range(HPG):
            spread[gi, h, h * HEAD_DIM + dd, gi * HEAD_DIM + dd] = 1.0
    collect = jnp.asarray(spread.transpose(0, 1, 3, 2), BF16)
    spread = jnp.asarray(spread, BF16)
    cols = HPG * tq
    kv_lane = lambda blk: pl.BlockSpec((s, LANES), lambda bi, gi, i: (bi, blk + gi // 2))
    cspec = lambda slot: pl.BlockSpec((1, 1, 1, LANES, LANES), lambda bi, gi, i: (slot, bi, gi // 2, 0, 0))
    return pl.pallas_call(
        functools.partial(_nsa_prompt_kernel, n_cmp=n_cmp, n_sel=n_sel, n_top=n_top),
        grid=(b, N_KV, nq),
        in_specs=[pl.BlockSpec((tq, HPG * HEAD_DIM), lambda bi, gi, i: (bi * nq + i, gi)),
                  cspec(0), cspec(1), kv_lane(4), kv_lane(6), kv_lane(0), kv_lane(2),
                  pl.BlockSpec((1, 1, LANES, cols), lambda bi, gi, i: (gi, i, 0, 0)),
                  pl.BlockSpec((1,) + btile.shape[1:], lambda bi, gi, i: (gi, 0, 0)),
                  _full(ovt.shape),
                  pl.BlockSpec((1,) + spread.shape[1:], lambda bi, gi, i: (gi % 2, 0, 0, 0)),
                  pl.BlockSpec((1,) + collect.shape[1:], lambda bi, gi, i: (gi % 2, 0, 0, 0))],
        out_specs=pl.BlockSpec((3, tq, HPG * HEAD_DIM), lambda bi, gi, i: (0, bi * nq + i, gi)),
        out_shape=jax.ShapeDtypeStruct((3, b * s, N_HEADS * HEAD_DIM), BF16),
        scratch_shapes=[pltpu.VMEM((LANES, tq), F32), pltpu.VMEM((8, cols), F32), pltpu.VMEM((8, cols), F32),
                        pltpu.VMEM((LANES, cols), F32)],
        compiler_params=_params(("parallel", "parallel", "arbitrary")),
        name="nsa_prompt",
    )(q, kcv, kcv, kvc, kvc, kvw, kvw, bias_cmp, btile, ovt, spread, collect)


def _nsa_sample_kernel(pt_ref, *refs, n_pages, n_sel, n_top):
    pages = refs[:n_pages]
    (q_ref, kvcn_ref, kvwn_ref, swin_ref, wq_ref, pet_ref, w2p_ref, bct_ref, bst_ref, bwt_ref, mwt_ref, mnt_ref,
     ovt_ref, hsum_ref, hm_ref, fold_ref, foldt_ref, selq_ref, o_ref, s_ref, rows_ref) = refs[n_pages:]
    del pt_ref
    gd = N_KV * HEAD_DIM
    nrow = n_pages * (LANES // CMP_STRIDE)
    n_chunk = pages[0].shape[2] // LANES
    for k, pg in enumerate(pages):
        for c in range(n_chunk):
            rows_ref[c, k * LANES:(k + 1) * LANES, :] = pg[0, 0, c * LANES:(c + 1) * LANES, :].T
    q_rep = jnp.concatenate([q_ref[0].astype(F32)] * N_HEADS, axis=0) * hm_ref[...]
    qr = _dot(q_rep.astype(BF16), fold_ref[...]).astype(BF16)

    def to_tokens(o):
        x = _dot(o.astype(BF16), foldt_ref[...]) * hm_ref[...]
        return _dot(selq_ref[...], x.astype(BF16)).astype(BF16)

    def page_cols(k, slot):
        return jnp.concatenate([rows_ref[2 * slot + h, k * LANES:(k + 1) * LANES, :] for h in range(2)], axis=1)

    kc_all = []
    for slot in range(2):
        out = jnp.zeros((nrow, gd), F32)
        for pair in range(2):
            chunk = 2 * slot + pair

            def tap(r, chunk=chunk):
                return rows_ref[chunk, pl.ds(r, nrow, stride=CMP_STRIDE), :]

            acc = _half_block_products(tap, wq_ref, slot, nrow)
            for gi in range(2):
                out = out + _block_summaries(acc, gi, pet_ref[slot, 0:1], w2p_ref[slot, 2 * pair + gi], nrow)
        kc_all.append(out.astype(BF16))

    cols = LANES

    def softmax_t(s):
        m = jnp.max(s, axis=0, keepdims=True)
        e = jnp.exp(s - m)
        return e / jnp.sum(e, axis=0, keepdims=True)

    def as_column(row):
        return jnp.broadcast_to(row, (cols, cols)).T[:, 0:1]

    n_cmp = nrow - 1
    c_io = lax.broadcasted_iota(I32, (nrow, cols), 0)
    s = _dot_nt(kc_all[0], qr) + bct_ref[...]
    valid = c_io < n_cmp
    p = jnp.where(valid, softmax_t(jnp.where(valid, s, NEG)), 0.0)
    o_ref[0, 0] = to_tokens(_dot_tn(p.astype(BF16), kc_all[1]))

    pg_sum = _dot_hilo_l(p, hsum_ref[...])
    g_hi, g_lo = _hilo(pg_sum)
    imp = _dot(ovt_ref[...], g_hi) + _dot(ovt_ref[...], g_lo)
    j_io = lax.broadcasted_iota(I32, imp.shape, 0)
    qblk = n_sel - 1
    forced = (j_io == 0) | (j_io == qblk) | (j_io == qblk - 1)
    score = jnp.where(forced, FORCE, jnp.where(j_io <= qblk, imp, NEG))
    score = jnp.where(j_io < n_sel, score, NEG_PAD)
    sel = _top_blocks(score, n_sel, n_top)

    per_page = LANES // SEL_LEN
    for k in range(n_pages):
        kp = page_cols(k, 2).astype(BF16)
        sk = _dot_nt(kp, qr) + bst_ref[k * LANES:(k + 1) * LANES, :]
        mk = jnp.concatenate([jnp.broadcast_to(sel[per_page * k + t:per_page * k + t + 1, :], (SEL_LEN, cols))
                              for t in range(per_page)], axis=0)
        s_ref[k * LANES:(k + 1) * LANES, :] = jnp.where(mk > 0.5, sk, NEG)
    zpad = jnp.zeros((LANES - 8, gd), F32)
    kn = jnp.concatenate([kvcn_ref[0][:, 2 * gd:3 * gd], zpad], axis=0).astype(BF16)
    vn = jnp.concatenate([kvcn_ref[0][:, 3 * gd:4 * gd], zpad], axis=0).astype(BF16)
    sn = _dot_nt(kn, qr) + bst_ref[n_pages * LANES:(n_pages + 1) * LANES, :]
    mn = (mnt_ref[...] > 0.5) & (jnp.broadcast_to(sel[n_sel - 1:n_sel, :], (LANES, cols)) > 0.5)
    s_ref[n_pages * LANES:(n_pages + 1) * LANES, :] = jnp.where(mn, sn, NEG)
    n_keys = (n_pages + 1) * LANES
    m = jnp.max(s_ref[0:n_keys, :], axis=0, keepdims=True)
    den = jnp.zeros((1, cols), F32)
    o = jnp.zeros((cols, gd), F32)
    for k in range(n_pages + 1):
        pk = jnp.exp(s_ref[k * LANES:(k + 1) * LANES, :] - m)
        den = den + jnp.sum(pk, axis=0, keepdims=True)
        vk = vn if k == n_pages else page_cols(k, 3).astype(BF16)
        o = o + _dot_tn(pk.astype(BF16), vk)
    o_ref[1, 0] = to_tokens(o / as_column(den))

    w_buf = swin_ref.shape[3]
    win_rows = swin_ref[0, 0].T
    kw = win_rows[:, 0:gd].astype(BF16)
    vw = win_rows[:, gd:2 * gd].astype(BF16)
    kwn = jnp.concatenate([kvwn_ref[0][:, 0:gd], zpad], axis=0).astype(BF16)
    vwn = jnp.concatenate([kvwn_ref[0][:, gd:2 * gd], zpad], axis=0).astype(BF16)
    sw = jnp.concatenate([_dot_nt(kw, qr), _dot_nt(kwn, qr)], axis=0) + bwt_ref[...]
    sw = jnp.where(mwt_ref[...] > 0.5, sw, NEG)
    mw = jnp.max(sw, axis=0, keepdims=True)
    pw = jnp.exp(sw - mw)
    denw = jnp.sum(pw, axis=0, keepdims=True)
    ow = _dot_tn(pw[0:w_buf].astype(BF16), vw) + _dot_tn(pw[w_buf:].astype(BF16), vwn)
    o_ref[2, 0] = to_tokens(ow / as_column(denw))


def _nsa_sample(layer, page_flat, cache4, q_s, kvcn, kvwn, swin, wq, pet, w2p,
                bct, bst, bwt, mwt, mnt, ovt, hsum, n_sel):
    bs, n_new, d = q_s.shape
    r_head = np.arange(LANES) // n_new
    c_head = np.arange(d) // HEAD_DIM
    hm = jnp.asarray((r_head[:, None] == c_head[None]).astype(np.float32))
    fold_np = np.zeros((d, N_KV * HEAD_DIM), np.float32)
    fold_np[np.arange(d), (c_head // HPG) * HEAD_DIM + np.arange(d) % HEAD_DIM] = 1.0
    fold, foldt = jnp.asarray(fold_np, BF16), jnp.asarray(fold_np.T, BF16)
    selq = jnp.asarray((np.arange(n_new)[:, None] == (np.arange(LANES) % n_new)[None]).astype(np.float32), BF16)
    n_pages = page_flat.shape[0] // bs
    w_buf = swin.shape[3]
    n_top = min(SEL_TOP, n_sel)
    page_specs = [pl.BlockSpec((1, 1, cache4.shape[2], LANES),
                               functools.partial(lambda b, pt, k: (layer, pt[b * n_pages + k], 0, 0), k=k))
                  for k in range(n_pages)]
    cfull = lambda a: pl.BlockSpec(a.shape, lambda b, pt: (0,) * a.ndim)
    in_specs = page_specs + [
        pl.BlockSpec((1, n_new, d), lambda b, pt: (b, 0, 0)),
        pl.BlockSpec((1,) + kvcn.shape[1:], lambda b, pt: (b, 0, 0)),
        pl.BlockSpec((1,) + kvwn.shape[1:], lambda b, pt: (b, 0, 0)),
        pl.BlockSpec((1, 1, swin.shape[2], w_buf), lambda b, pt: (layer, b, 0, 0)),
        cfull(wq), cfull(pet), cfull(w2p), cfull(bct), cfull(bst), cfull(bwt), cfull(mwt), cfull(mnt),
        cfull(ovt), cfull(hsum), cfull(hm), cfull(fold), cfull(foldt), cfull(selq)]
    args = [cache4] * n_pages + [q_s, kvcn, kvwn, swin, wq, pet, w2p, bct, bst, bwt, mwt, mnt, ovt, hsum,
                                 hm, fold, foldt, selq]
    grid_spec = pltpu.PrefetchScalarGridSpec(
        num_scalar_prefetch=1, grid=(bs,), in_specs=in_specs,
        out_specs=pl.BlockSpec((3, 1, n_new, d), lambda b, pt: (0, b, 0, 0)),
        scratch_shapes=[pltpu.VMEM(((n_pages + 1) * LANES, LANES), F32),
                        pltpu.VMEM((cache4.shape[2] // LANES, n_pages * LANES, LANES), F32)])
    return pl.pallas_call(
        functools.partial(_nsa_sample_kernel, n_pages=n_pages, n_sel=n_sel, n_top=n_top), grid_spec=grid_spec,
        out_shape=jax.ShapeDtypeStruct((3, bs, n_new, d), BF16),
        compiler_params=_params(("arbitrary",)),
        name="nsa_sample",
    )(page_flat, *args)


def _merge_kernel(x_ref, gu_ref, v_ref, o3p_ref, o3s_ref, gn_ref, gm_ref, wmix_ref, bmix_ref, ex_ref, wb_ref,
                  wo_ref, nf_ref, wrh_ref, wrl_ref, br_ref, x1_o, h2_o, ei_o, rw_o, *, n_prompt_tiles):
    tm = x_ref.shape[0]
    is_prompt = pl.program_id(0) < n_prompt_tiles
    mixed = []
    for c in range(tm // CHUNK):
        vb = v_ref[c * CHUNK:(c + 1) * CHUNK, :].astype(BF16)
        mixed.append(jnp.concatenate(
            [_dot(wmix_ref[0, g], vb[:, g * A_GROUP_WIDTH:(g + 1) * A_GROUP_WIDTH]) for g in range(A_GROUPS)],
            axis=1) + bmix_ref[0])
    o_a = gu_ref[...].astype(F32) * jnp.concatenate(mixed, axis=0)
    gn = gn_ref[...]
    o_b = jnp.zeros((tm, D_MODEL), F32)
    for br in range(3):
        o_br = jnp.where(is_prompt, o3p_ref[br], o3s_ref[br])
        o_b = o_b + _dot_hilo_l(gn, ex_ref[br]) * o_br.astype(F32)
    gm = gm_ref[...].astype(F32)
    merged = (gm[:, :D_MODEL] * _dot(o_a.astype(BF16), wb_ref[0])
              + gm[:, D_MODEL:] * _dot(o_b.astype(BF16), wb_ref[1]))
    x1 = x_ref[...] + _dot(merged.astype(BF16), wo_ref[...])
    x1_o[...] = x1
    h2 = _rms(x1, nf_ref[...])
    h2_o[...] = h2
    hh, hl = _hilo(h2)
    logit = _dot(hh, wrh_ref[...]) + _dot(hl, wrh_ref[...]) + _dot(hh, wrl_ref[...]) + br_ref[...]
    lane = lax.broadcasted_iota(I32, logit.shape, 1)
    big = jnp.int32(9999)
    is_g = lane < N_GROUPS
    gl = jnp.where(is_g, logit, -jnp.inf)
    gmax = jnp.max(gl, axis=-1, keepdims=True)
    grp = jnp.min(jnp.where(gl == gmax, lane, big), axis=-1, keepdims=True)
    p_grp = 1.0 / jnp.sum(jnp.where(is_g, jnp.exp(logit - gmax), 0.0), axis=-1, keepdims=True)
    e_lane = lane - N_GROUPS
    in_grp = (e_lane >= 0) & (lax.shift_right_arithmetic(e_lane, 3) == grp) & (e_lane < N_EXPERTS)
    el = jnp.where(in_grp, logit, -jnp.inf)
    t1 = jnp.max(el, axis=-1, keepdims=True)
    i1 = jnp.min(jnp.where(el == t1, lane, big), axis=-1, keepdims=True)
    el2 = jnp.where(lane == i1, -jnp.inf, el)
    t2 = jnp.max(el2, axis=-1, keepdims=True)
    i2 = jnp.min(jnp.where(el2 == t2, lane, big), axis=-1, keepdims=True)
    r = jnp.exp(t2 - t1)
    w1 = p_grp / (1.0 + r)
    w2 = p_grp * r / (1.0 + r)
    ei_o[...] = jnp.where(lane == 0, i1 - N_GROUPS, jnp.where(lane == 1, i2 - N_GROUPS, 0))
    rw_o[...] = jnp.where(lane == 0, w1, jnp.where(lane == 1, w2, 0.0))


def _merge(x, gu, v, o3p, o3s, gn, gm, wmix, bmix, ex, wb, wo, nf, wrh, wrl, brr, n_prompt_tiles):
    t = x.shape[0]
    tm = TOK_TILE
    row = lambda n: pl.BlockSpec((tm, n), lambda i: (i, 0))
    kind = lambda i: jnp.where(i < n_prompt_tiles, 0, 1)
    outs = [(D_MODEL, F32), (D_MODEL, F32), (LANES, I32), (LANES, F32)]
    return pl.pallas_call(
        functools.partial(_merge_kernel, n_prompt_tiles=n_prompt_tiles),
        grid=(t // tm,),
        in_specs=[row(D_MODEL), row(A_WIDTH), row(A_WIDTH),
                  pl.BlockSpec((3, tm, D_MODEL), lambda i: (0, jnp.minimum(i, n_prompt_tiles - 1), 0)),
                  pl.BlockSpec((3, tm, D_MODEL), lambda i: (0, jnp.maximum(i - n_prompt_tiles, 0), 0)),
                  row(LANES), row(2 * D_MODEL),
                  pl.BlockSpec((1, A_GROUPS, CHUNK, CHUNK), lambda i: (kind(i), 0, 0, 0)),
                  pl.BlockSpec((1, CHUNK, A_WIDTH), lambda i: (kind(i), 0, 0)),
                  _full(ex.shape), _full(wb.shape), _full(wo.shape), _full((1, D_MODEL)),
                  _full(wrh.shape), _full(wrl.shape), _full((1, LANES))],
        out_specs=[row(n) for n, _ in outs],
        out_shape=[jax.ShapeDtypeStruct((t, n), d) for n, d in outs],
        compiler_params=_params(("parallel",)),
        name="merge",
    )(x, gu, v, o3p, o3s, gn, gm, wmix, bmix, ex, wb, wo, nf.reshape(1, -1), wrh, wrl, brr)


def _route1_kernel(ei_ref, ltri_ref, rank_o, cnt_o, carry):
    @pl.when(pl.program_id(0) == 0)
    def _():
        carry[...] = jnp.zeros(carry.shape, F32)

    ei = ei_ref[...]
    lane = lax.broadcasted_iota(I32, ei.shape, 1)
    e1, e2 = ei[:, 0:1], ei[:, 1:2]
    oh = jnp.where((lane == e1) | (lane == e2), 1.0, 0.0)
    cum = _dot(ltri_ref[...], oh.astype(BF16)) + carry[0:1, :]
    r1 = jnp.sum(jnp.where(lane == e1, cum, 0.0), axis=-1, keepdims=True)
    r2 = jnp.sum(jnp.where(lane == e2, cum, 0.0), axis=-1, keepdims=True)
    rank_o[...] = jnp.where(lane == 0, r1, jnp.where(lane == 1, r2, 0.0)).astype(I32)
    carry[...] = carry[...] + jnp.sum(oh, axis=0, keepdims=True)
    cnt_o[...] = carry[...]


def _route2_kernel(ei_ref, rank_ref, cnt_ref, utri_ref, dest_o, be_o, nu_o):
    shift = int(math.log2(MOE_BLOCK))
    nb = lax.shift_right_logical(cnt_ref[...].astype(I32) + (MOE_BLOCK - 1), shift).astype(F32)
    start = _dot(nb.astype(BF16), utri_ref[...])
    ei = ei_ref[...]
    lane = lax.broadcasted_iota(I32, ei.shape, 1)
    e1, e2 = ei[:, 0:1], ei[:, 1:2]
    s1 = jnp.sum(jnp.where(lane == e1, start[0:1, :], 0.0), axis=-1, keepdims=True)
    s2 = jnp.sum(jnp.where(lane == e2, start[0:1, :], 0.0), axis=-1, keepdims=True)
    rk = rank_ref[...]
    d1 = s1.astype(I32) * MOE_BLOCK + rk[:, 0:1]
    d2 = s2.astype(I32) * MOE_BLOCK + rk[:, 1:2]
    dest_o[...] = jnp.where(lane == 0, d1, jnp.where(lane == 1, d2, 0))
    end = start[0:1, :] + nb[0:1, :]
    j = lax.broadcasted_iota(I32, be_o.shape, 0).astype(F32)
    l2 = lax.broadcasted_iota(I32, be_o.shape, 1)
    ge = jnp.sum(jnp.where((l2 < N_EXPERTS) & (end <= j), 1.0, 0.0), axis=-1, keepdims=True)
    be_o[...] = jnp.broadcast_to(jnp.minimum(ge, N_EXPERTS - 1.0).astype(I32), be_o.shape)
    l3 = lax.broadcasted_iota(I32, nu_o.shape, 1)
    nu = jnp.sum(jnp.where(l3 == N_EXPERTS - 1, jnp.broadcast_to(end, nu_o.shape), 0.0), axis=-1, keepdims=True)
    nu_o[...] = jnp.broadcast_to(nu.astype(I32), nu_o.shape)


def _route(ei, n_blocks):
    t = ei.shape[0]
    tm = TOK_TILE
    row = pl.BlockSpec((tm, LANES), lambda i: (i, 0))
    ltri = jnp.asarray(np.tril(np.ones((tm, tm), np.float32), -1), BF16)
    utri = jnp.asarray(np.triu(np.ones((LANES, LANES), np.float32), 1), BF16)
    rank, cnt = pl.pallas_call(
        _route1_kernel, grid=(t // tm,),
        in_specs=[row, _full((tm, tm))],
        out_specs=[row, _full((8, LANES))],
        out_shape=[jax.ShapeDtypeStruct((t, LANES), I32), jax.ShapeDtypeStruct((8, LANES), F32)],
        scratch_shapes=[pltpu.VMEM((8, LANES), F32)],
        compiler_params=_params(("arbitrary",)),
        name="route_rank",
    )(ei, ltri)
    nbp = -(-n_blocks // 8) * 8
    dest, be, nu = pl.pallas_call(
        _route2_kernel, grid=(t // tm,),
        in_specs=[row, row, _full((8, LANES)), _full((LANES, LANES))],
        out_specs=[row, _full((nbp, LANES)), _full((8, LANES))],
        out_shape=[jax.ShapeDtypeStruct((t, LANES), I32), jax.ShapeDtypeStruct((nbp, LANES), I32),
                   jax.ShapeDtypeStruct((8, LANES), I32)],
        compiler_params=_params(("arbitrary",)),
        name="route_slots",
    )(ei, rank, cnt, utri)
    return dest[:, :2].reshape(-1), be[:n_blocks, 0], nu[0, :1]


def _dispatch_kernel(dest_ref, h_ref, xs_in, xs_out, sem, *, chunk):
    del xs_in
    i = pl.program_id(0)

    def row_copy(r, d):
        return pltpu.make_async_copy(h_ref.at[pl.ds(r, 1)], xs_out.at[pl.ds(d, 1)], sem)

    def issue(r, c):
        t = i * chunk + r
        row_copy(r, dest_ref[2 * t]).start()
        row_copy(r, dest_ref[2 * t + 1]).start()
        return c

    lax.fori_loop(0, chunk, issue, 0, unroll=8)
    for _ in range(2):
        pltpu.make_async_copy(h_ref, xs_out.at[pl.ds(0, chunk)], sem).wait()


def _dispatch(dest_flat, h2, cap):
    t, d = h2.shape
    chunk = TOK_TILE
    grid_spec = pltpu.PrefetchScalarGridSpec(
        num_scalar_prefetch=1, grid=(t // chunk,),
        in_specs=[pl.BlockSpec((chunk, d), lambda i, dr: (i, 0)), pl.BlockSpec(memory_space=pl.ANY)],
        out_specs=pl.BlockSpec(memory_space=pl.ANY),
        scratch_shapes=[pltpu.SemaphoreType.DMA(())])
    return pl.pallas_call(
        functools.partial(_dispatch_kernel, chunk=chunk), grid_spec=grid_spec,
        out_shape=jax.ShapeDtypeStruct((cap, d), h2.dtype),
        input_output_aliases={2: 0},
        compiler_params=_params(("arbitrary",)),
        name="moe_dispatch",
    )(dest_flat, h2, jnp.zeros((cap, d), h2.dtype))


def _expert_kernel(be_ref, nu_ref, x_ref, wg_ref, wu_ref, wd_ref, y_ref):
    i = pl.program_id(0)

    @pl.when(i < nu_ref[0])
    def _():
        xb = x_ref[...].astype(BF16)
        g = _dot(xb, wg_ref[0])
        u = _dot(xb, wu_ref[0])
        a = (g * _sigmoid(g) * u).astype(BF16)
        y_ref[...] = _dot(a, wd_ref[0])

    @pl.when(i >= nu_ref[0])
    def _():
        y_ref[...] = jnp.zeros(y_ref.shape, F32)


def _experts(be, nu, xs, wg, wu, wd):
    cap, d = xs.shape
    bm = MOE_BLOCK
    xin = lambda i, be, nu: (jnp.minimum(i, jnp.maximum(nu[0] - 1, 0)), 0)
    wsel = lambda i, be, nu: (be[i], 0, 0)
    grid_spec = pltpu.PrefetchScalarGridSpec(
        num_scalar_prefetch=2, grid=(cap // bm,),
        in_specs=[pl.BlockSpec((bm, d), xin),
                  pl.BlockSpec((1, d, D_EXPERT), wsel), pl.BlockSpec((1, d, D_EXPERT), wsel),
                  pl.BlockSpec((1, D_EXPERT, d), wsel)],
        out_specs=pl.BlockSpec((bm, d), lambda i, be, nu: (i, 0)))
    return pl.pallas_call(
        _expert_kernel, grid_spec=grid_spec,
        out_shape=jax.ShapeDtypeStruct((cap, d), F32),
        compiler_params=_params(("arbitrary",)),
        name="moe_experts",
    )(be, nu, xs, wg, wu, wd)


def _combine_kernel(dest_ref, x_ref, rw_ref, pe_ref, y_hbm, np_ref, wg_ref, wp_ref, fn_ref, o_ref, ybuf, sem,
                    *, final):
    tm = x_ref.shape[0]
    i = pl.program_id(0)
    n = pl.num_programs(0)

    def row_copy(slot, k, r, d):
        return pltpu.make_async_copy(y_hbm.at[pl.ds(d, 1)], ybuf.at[slot, k, pl.ds(r, 1)], sem.at[slot])

    def issue(tile, slot):
        def body(r, c):
            t = tile * tm + r
            row_copy(slot, 0, r, dest_ref[2 * t]).start()
            row_copy(slot, 1, r, dest_ref[2 * t + 1]).start()
            return c
        lax.fori_loop(0, tm, body, 0, unroll=8)

    slot = i & 1

    @pl.when(i == 0)
    def _():
        issue(0, 0)

    @pl.when(i + 1 < n)
    def _():
        issue(i + 1, 1 - slot)

    for k in range(2):
        pltpu.make_async_copy(y_hbm.at[pl.ds(0, tm)], ybuf.at[slot, k], sem.at[slot]).wait()

    rw = rw_ref[...]
    x2 = x_ref[...] + rw[:, 0:1] * ybuf[slot, 0] + rw[:, 1:2] * ybuf[slot, 1]
    gate = _sigmoid(_dot(_rms(x2, np_ref[...]).astype(BF16), wg_ref[...]))
    x3 = x2 + gate * _dot(pe_ref[...].astype(BF16), wp_ref[...])
    o_ref[...] = _rms(x3, fn_ref[...]) if final else x3


def _combine(dest_flat, x1, rw, pemb, y, norm_ple, wg, wp, final_norm, final):
    t, d = x1.shape
    tm = TOK_TILE
    row = lambda n: pl.BlockSpec((tm, n), lambda i, dr: (i, 0))
    cfull = lambda shape: pl.BlockSpec(shape, lambda i, dr: (0,) * len(shape))
    grid_spec = pltpu.PrefetchScalarGridSpec(
        num_scalar_prefetch=1, grid=(t // tm,),
        in_specs=[row(d), row(LANES), row(PLE_DIM), pl.BlockSpec(memory_space=pl.ANY),
                  cfull((1, d)), cfull(wg.shape), cfull(wp.shape), cfull((1, d))],
        out_specs=row(d),
        scratch_shapes=[pltpu.VMEM((2, 2, tm, d), F32), pltpu.SemaphoreType.DMA((2,))])
    return pl.pallas_call(
        functools.partial(_combine_kernel, final=final), grid_spec=grid_spec,
        out_shape=jax.ShapeDtypeStruct((t, d), F32),
        compiler_params=_params(("arbitrary",)),
        name="moe_combine_ple",
    )(dest_flat, x1, rw, pemb, y, norm_ple.reshape(1, -1), wg, wp, final_norm.reshape(1, -1))


def _per_head_lookup(rel_bias, bkt):
    onehot = (jnp.arange(N_BUCKETS, dtype=I32)[:, None] == jnp.asarray(bkt.reshape(1, -1), I32)).astype(F32)
    tab = jnp.dot(rel_bias.T, onehot, precision=lax.Precision.HIGHEST)
    return tab.reshape((N_HEADS,) + bkt.shape)


def _prompt_tables(rel_bias, s):
    tq = ATT_TILE
    nq = s // tq
    n_cmp = (s - CMP_LEN) // CMP_STRIDE + 1
    n_sel = -(-s // SEL_LEN)
    c = np.arange(LANES)
    qpos = np.arange(s).reshape(nq, 1, tq)
    bkt = _bucket_np(qpos - (c * CMP_STRIDE + CMP_LEN - 1)[None, :, None])
    bias_cmp = _per_head_lookup(rel_bias, bkt).reshape(N_KV, HPG, nq, LANES, tq)
    bias_cmp = bias_cmp.transpose(0, 2, 3, 1, 4).reshape(N_KV, nq, LANES, HPG * tq)
    u, ql = np.arange(2 * BIAS_ZERO_ROW)[:, None], np.arange(tq)[None]
    assert (_bucket_np(np.arange(BIAS_ZERO_ROW - 4 * tq + 1, 2 * s)) == N_BUCKETS - 1).all()
    btile = _per_head_lookup(rel_bias, _bucket_np(ql - u + BIAS_ZERO_ROW)).reshape(N_KV, HPG, 2 * BIAS_ZERO_ROW, tq)
    btile = btile.transpose(0, 2, 1, 3).reshape(N_KV, 2 * BIAS_ZERO_ROW, HPG * tq)
    cs, ss = np.arange(n_cmp) * CMP_STRIDE, np.arange(n_sel) * SEL_LEN
    overlap = ((cs[:, None] < ss[None] + SEL_LEN) & (cs[:, None] + CMP_LEN > ss[None])).astype(np.float32)
    ovt = np.zeros((LANES, LANES), np.float32)
    ovt[:n_sel, :n_cmp] = overlap.T
    return bias_cmp, btile, jnp.asarray(ovt, BF16)


def _per_column_lookup(rel_bias, bkt, n_new):
    rows = bkt.shape[0]
    onehot = (jnp.asarray(np.repeat(bkt, N_BUCKETS, axis=1), I32)
              == jnp.asarray(np.tile(np.arange(N_BUCKETS), n_new)[None], I32)).astype(F32)
    spread = (rel_bias[None, :, :, None] * jnp.eye(n_new, dtype=F32)[:, None, None, :])
    spread = spread.reshape(n_new * N_BUCKETS, N_HEADS * n_new)
    return jnp.dot(onehot, spread, precision=lax.Precision.HIGHEST).reshape(rows, N_HEADS * n_new)


def _sample_tables(rel_bias, past, n_new, w_buf):
    cols = np.arange(LANES)
    qpos = past + cols % n_new
    qnew = past + np.arange(n_new)
    n_pages = past // LANES
    n_cmp = (past + n_new - CMP_LEN) // CMP_STRIDE + 1
    n_sel = -(-(past + n_new) // SEL_LEN)
    assert n_cmp == past // CMP_STRIDE - 1 and HPG * n_new * N_KV == LANES
    c = np.arange(past // CMP_STRIDE)
    bct = _per_column_lookup(rel_bias, _bucket_np(qnew[None] - (c * CMP_STRIDE + CMP_LEN - 1)[:, None]), n_new)
    key = np.arange((n_pages + 1) * LANES)
    bst = _per_column_lookup(rel_bias, _bucket_np(qnew[None] - key[:, None]), n_new)
    kw = np.arange(w_buf + LANES)
    kpos = np.where(kw < w_buf, past - w_buf + kw, past + kw - w_buf)
    bwt = _per_column_lookup(rel_bias, _bucket_np(qnew[None] - kpos[:, None]), n_new)
    dw = qpos[None] - kpos[:, None]
    mwt = ((dw >= 0) & (dw < WINDOW) & (kpos[:, None] >= 0) & (kw[:, None] < w_buf + n_new)).astype(np.float32)
    kn = np.arange(LANES)
    mnt = ((kn[:, None] < n_new) & (kn[:, None] <= (cols % n_new)[None])).astype(np.float32)
    cs, ss = np.arange(n_cmp) * CMP_STRIDE, np.arange(n_sel) * SEL_LEN
    overlap = ((cs[:, None] < ss[None] + SEL_LEN) & (cs[:, None] + CMP_LEN > ss[None])).astype(np.float32)
    rows_sel = -(-n_sel // 8) * 8
    ovt = np.zeros((rows_sel, past // CMP_STRIDE), np.float32)
    ovt[:n_sel, :n_cmp] = overlap.T
    same = (cols[:, None] // (HPG * n_new) == cols[None] // (HPG * n_new)) & \
           (cols[:, None] % n_new == cols[None] % n_new)
    return (bct, bst, bwt, jnp.asarray(mwt), jnp.asarray(mnt), jnp.asarray(ovt, BF16),
            jnp.asarray(same.astype(np.float32), BF16), n_sel)


def kernel(x_prompt, x_sample, cache_kv, state_win_kv, page_table, p_prompt, p_sample, rel_bias, norm_mix, w_in, v_norm, w_spatial, b_spatial, pe_cmp, w_phi1, w_phi2, w_branch, w_out, norm_ffn, w_router_group, b_router_group, w_router_expert, b_router_expert, w_exp_gate, w_exp_up, w_exp_down, norm_ple, w_ple_gate, w_ple_proj, final_norm):
    b, s, d = x_prompt.shape
    bs, n_new, _ = x_sample.shape
    depth = w_in.shape[0]
    n_pool, page = cache_kv.shape[1], cache_kv.shape[2]
    past = page_table.shape[1] * page
    w_buf = state_win_kv.shape[2]
    tp, ts = b * s, bs * n_new
    t = tp + ts
    assert page == LANES and tp % TOK_TILE == 0 and ts % TOK_TILE == 0 and n_new == 8 and s >= CHUNK

    x = jnp.concatenate([x_prompt.reshape(tp, d), x_sample.reshape(ts, d)], axis=0)
    pemb = jnp.concatenate([p_prompt.reshape(depth, tp, PLE_DIM), p_sample.reshape(depth, ts, PLE_DIM)], axis=1)
    cache4 = cache_kv.transpose(0, 1, 3, 4, 5, 2).reshape(depth, n_pool, -1, page)
    swin = state_win_kv.transpose(0, 1, 3, 4, 5, 2).reshape(depth, bs, -1, w_buf)
    page_flat = page_table.reshape(-1).astype(I32)

    bias_cmp, btile, ovt_p = _prompt_tables(rel_bias, s)
    bct, bst, bwt, mwt, mnt, ovt_s, hsum, n_sel_s = _sample_tables(rel_bias, past, n_new, w_buf)

    ex = np.zeros((3, LANES, D_MODEL), np.float32)
    for br in range(3):
        ex[br, br * N_HEADS + np.arange(D_MODEL) // HEAD_DIM, np.arange(D_MODEL)] = 1.0
    ex = jnp.asarray(ex, BF16)
    tril = np.tril(np.ones((CHUNK, CHUNK), np.float32))
    blockdiag = np.kron(np.eye(CHUNK // n_new, dtype=np.float32), np.tril(np.ones((n_new, n_new), np.float32)))
    eye_g = jnp.eye(N_KV, dtype=BF16)
    eye_2 = jnp.eye(2, dtype=BF16)

    n_blocks = -(-2 * t // MOE_BLOCK) + N_EXPERTS
    cap = n_blocks * MOE_BLOCK

    kv_p, kv_s, win_p, win_s, v_s = [], [], [], [], []
    for i in range(depth):
        gu, v, q, kvc, kvw, gn, gm = _inproj(x, norm_mix[i], w_in[i], v_norm[i])
        kv_p.append(kvc[:tp].reshape(b, s, 4, N_KV, HEAD_DIM))
        kv_s.append(kvc[tp:].reshape(bs, n_new, 4, N_KV, HEAD_DIM))
        win_p.append(kvw[:tp].reshape(b, s, 2, N_KV, HEAD_DIM)[:, s - min(WINDOW, s):])
        win_s.append(jnp.concatenate([state_win_kv[i][:, n_new:], kvw[tp:].reshape(bs, n_new, 2, N_KV, HEAD_DIM)],
                                     axis=1))
        v_s.append(v[tp:].reshape(bs, n_new, A_WIDTH))

        w1b = w_phi1[i].astype(BF16)
        w2b = w_phi2[i].astype(BF16)
        pe_flat = jnp.broadcast_to(pe_cmp[i].reshape(2, 1, CMP_LEN * HEAD_DIM), (2, 8, CMP_LEN * HEAD_DIM))
        w1r = w1b.reshape(2, 2, CMP_STRIDE, HEAD_DIM, CMP_HIDDEN)
        wq = jnp.einsum('sprdn,ij->sridjpn', w1r, eye_2).reshape(2, CMP_STRIDE * LANES, 4 * CMP_HIDDEN)
        w2p = jnp.einsum('skd,gh->sgkhd', w2b, eye_g).reshape(2, N_KV, CMP_HIDDEN, N_KV * HEAD_DIM)
        w2pair = jnp.einsum('skd,gh->sgkhd', w2b, eye_2).reshape(2, 2, CMP_HIDDEN, LANES)

        kcv, pet = _cmp_prompt(kvc, b, s, pe_flat.astype(BF16), w1b, wq, w2pair)
        o3_p = _nsa_prompt(q, kcv, kvc, kvw, b, bias_cmp, btile, ovt_p)

        o3_s = _nsa_sample(i, page_flat, cache4, q[tp:].reshape(bs, n_new, d),
                           kvc[tp:].reshape(bs, n_new, -1), kvw[tp:].reshape(bs, n_new, -1), swin,
                           wq, pet, w2p, bct, bst, bwt, mwt, mnt, ovt_s, hsum, n_sel_s).reshape(3, ts, d)

        ws = w_spatial[i]
        wmix = jnp.stack([ws * tril, jnp.tile(ws[:, :n_new, :n_new], (1, CHUNK // n_new, CHUNK // n_new)) * blockdiag])
        bsp = b_spatial[i]
        bmix = jnp.stack([jnp.repeat(bsp.T, A_GROUP_WIDTH, axis=1),
                          jnp.repeat(jnp.tile(bsp[:, :n_new], (1, CHUNK // n_new)).T, A_GROUP_WIDTH, axis=1)])
        wr = jnp.concatenate([w_router_group[i], w_router_expert[i]], axis=1)
        wr = jnp.pad(wr, ((0, 0), (0, LANES - wr.shape[1])))
        wrh = wr.astype(BF16)
        wrl = (wr - wrh.astype(F32)).astype(BF16)
        brr = jnp.pad(jnp.concatenate([b_router_group[i], b_router_expert[i]]), (0, LANES - N_GROUPS - N_EXPERTS))
        x1, h2, ei, rw = _merge(x, gu, v, o3_p, o3_s, gn, gm, wmix.astype(BF16), bmix, ex, w_branch[i].astype(BF16),
                                w_out[i].astype(BF16), norm_ffn[i], wrh, wrl, brr.reshape(1, LANES),
                                tp // TOK_TILE)

        dest_flat, be, nu = _route(ei, n_blocks)
        xs = _dispatch(dest_flat, h2, cap)
        y = _experts(be, nu, xs, w_exp_gate[i].astype(BF16), w_exp_up[i].astype(BF16),
                     w_exp_down[i].astype(BF16))
        x = _combine(dest_flat, x1, rw, pemb[i], y, norm_ple[i], w_ple_gate[i].astype(BF16),
                     w_ple_proj[i].astype(BF16), final_norm, final=(i == depth - 1))

    y_prompt = x[:tp].reshape(b, s, d)
    y_sample = x[tp:].reshape(bs, n_new, d)
    return (y_prompt, y_sample, jnp.stack(kv_p), jnp.stack(kv_s), jnp.stack(win_p), jnp.stack(win_s),
            jnp.stack(v_s))
```

```python
import functools
import math

import numpy as np
import jax
import jax.numpy as jnp
from jax import lax
from jax.experimental import pallas as pl
from jax.experimental.pallas import tpu as pltpu

F32 = jnp.float32
BF16 = jnp.bfloat16
I32 = jnp.int32

D_MODEL = 1024
A_WIDTH = 1024
A_GROUPS = 4
A_GROUP_WIDTH = A_WIDTH // A_GROUPS
CHUNK = 128
N_HEADS = 16
HEAD_DIM = 64
N_KV = 4
HPG = N_HEADS // N_KV
CMP_LEN = 32
CMP_STRIDE = 16
CMP_HIDDEN = 256
SEL_LEN = 64
SEL_TOP = 16
WINDOW = 512
N_BUCKETS = 32
MAX_DISTANCE = 128
N_GROUPS = 4
EXPERTS_PER_GROUP = 8
N_EXPERTS = 32
D_EXPERT = 512
PLE_DIM = 256
EPS = 1e-6
NEG = -1e30
FORCE = 1e9
NEG_PAD = -3e38

LANES = 128
TOK_TILE = 256
ATT_TILE = 128
MOE_BLOCK = 256
BIAS_ZERO_ROW = WINDOW + ATT_TILE
VMEM_LIMIT = 56 * 1024 * 1024


def _dot(a, b):
    return jnp.dot(a, b, preferred_element_type=F32)


def _dot_nt(a, b):
    return lax.dot_general(a, b, (((1,), (1,)), ((), ())), preferred_element_type=F32)


def _dot_tn(a, b):
    return lax.dot_general(a, b, (((0,), (0,)), ((), ())), preferred_element_type=F32)


def _hilo(a):
    hi = a.astype(BF16)
    lo = (a - hi.astype(F32)).astype(BF16)
    return hi, lo


def _dot_hilo_l(a, b):
    hi, lo = _hilo(a)
    return _dot(hi, b) + _dot(lo, b)


def _gelu(x):
    return 0.5 * x * (1.0 + jnp.tanh(0.7978845608028654 * (x + 0.044715 * (x * x * x))))


def _sigmoid(x):
    return 1.0 / (1.0 + jnp.exp(-x))


def _rms(x, gain):
    return x * lax.rsqrt(jnp.mean(x * x, axis=-1, keepdims=True) + EPS) * gain


def _full(shape):
    nd = len(shape)
    return pl.BlockSpec(shape, lambda *_: (0,) * nd)


def _params(sem, vmem=VMEM_LIMIT):
    return pltpu.CompilerParams(dimension_semantics=sem, vmem_limit_bytes=vmem)


def _bucket_np(dist):
    n = np.maximum(np.asarray(dist, np.int64), 0)
    max_exact = N_BUCKETS // 2
    nf = np.maximum(n, max_exact).astype(np.float64)
    large = max_exact + (np.log(nf / max_exact) / math.log(MAX_DISTANCE / max_exact)
                         * (N_BUCKETS - max_exact)).astype(np.int64)
    return np.where(n < max_exact, n, np.minimum(large, N_BUCKETS - 1)).astype(np.int32)


def _inproj_kernel(x_ref, g_ref, wu, wv, wq, wkc, wkw, wgn, wgm, vn_ref,
                   gu_o, v_o, q_o, kvc_o, kvw_o, gn_o, gm_o):
    x = x_ref[...]
    hb = _rms(x, g_ref[...]).astype(BF16)
    gu_o[...] = _gelu(_dot(hb, wu[...])).astype(BF16)
    v_o[...] = _rms(_gelu(_dot(hb, wv[...])), vn_ref[...])
    q_o[...] = (_dot(hb, wq[...]) * (HEAD_DIM ** -0.5)).astype(BF16)
    kvc_o[...] = _dot(hb, wkc[...])
    kvw_o[...] = _dot(hb, wkw[...])
    gn_o[...] = _sigmoid(_dot(hb, wgn[...]))
    gm_o[...] = _sigmoid(_dot(hb, wgm[...])).astype(BF16)


def _inproj(x, gain, w_in, v_gain):
    t = x.shape[0]
    tm = TOK_TILE
    a = A_WIDTH
    c_q, c_kv, c_gn, c_gm = 2 * a, 3 * a, 3 * a + 1536, 3 * a + 1536 + 48
    wb = w_in.astype(BF16)
    wu, wv, wq = wb[:, :a], wb[:, a:2 * a], wb[:, c_q:c_kv]
    wkc, wkw = wb[:, c_kv:c_kv + 1024], wb[:, c_kv + 1024:c_gn]
    wgn = jnp.pad(wb[:, c_gn:c_gm], ((0, 0), (0, LANES - 48)))
    wgm = wb[:, c_gm:]
    row = lambda n: pl.BlockSpec((tm, n), lambda i: (i, 0))
    outs = [(a, BF16), (a, F32), (a, BF16), (1024, F32), (512, F32), (LANES, F32), (2 * D_MODEL, BF16)]
    return pl.pallas_call(
        _inproj_kernel,
        grid=(t // tm,),
        in_specs=[row(D_MODEL), _full((1, D_MODEL)), _full(wu.shape), _full(wv.shape), _full(wq.shape),
                  _full(wkc.shape), _full(wkw.shape), _full(wgn.shape), _full(wgm.shape), _full((1, a))],
        out_specs=[row(n) for n, _ in outs],
        out_shape=[jax.ShapeDtypeStruct((t, n), d) for n, d in outs],
        compiler_params=_params(("parallel",)),
        name="inproj",
    )(x, gain.reshape(1, -1), wu, wv, wq, wkc, wkw, wgn, wgm, v_gain.reshape(1, -1))


def _half_block_products(tap, wq_ref, slot, nrow):
    acc = jnp.zeros((nrow, 4 * CMP_HIDDEN), F32)
    for r2 in range(CMP_STRIDE // 2):
        lhs = jnp.concatenate([tap(2 * r2), tap(2 * r2 + 1)], axis=1).astype(BF16)
        acc = acc + _dot(lhs, wq_ref[slot, r2 * 2 * LANES:(r2 + 1) * 2 * LANES, :])
    return acc


def _block_summaries(acc, gi, pe_row, w2, nrow):
    c0 = gi * 2 * CMP_HIDDEN
    pre = acc[:, c0:c0 + CMP_HIDDEN] + pltpu.roll(acc[:, c0 + CMP_HIDDEN:c0 + 2 * CMP_HIDDEN], nrow - 1, 0) + pe_row
    return _dot(_gelu(pre).astype(BF16), w2)


def _cmp_prompt_kernel(kv_ref, pe_ref, w1_ref, wq_ref, w2_ref, o_ref, pt_ref):
    nrow = kv_ref.shape[0] // CMP_STRIDE
    pe_term = _dot(pe_ref[0], w1_ref[0])
    acc = _half_block_products(lambda r: kv_ref[pl.ds(r, nrow, stride=CMP_STRIDE), :], wq_ref, 0, nrow)
    out = jnp.zeros((nrow, LANES), F32)
    for gi in range(2):
        out = out + _block_summaries(acc, gi, pe_term[0:1], w2_ref[0, gi], nrow)
    o_ref[0, 0, 0] = out.astype(BF16)
    pt_ref[0] = pe_term


def _cmp_prompt(kvc, b, s, pe_flat, w1, wq, w2pair):
    nrow = s // CMP_STRIDE
    return pl.pallas_call(
        _cmp_prompt_kernel,
        grid=(2, b, 2),
        in_specs=[pl.BlockSpec((s, LANES), lambda sl, i, p: (i, 2 * sl + p)),
                  pl.BlockSpec((1,) + pe_flat.shape[1:], lambda sl, i, p: (sl, 0, 0)),
                  pl.BlockSpec((1,) + w1.shape[1:], lambda sl, i, p: (sl, 0, 0)),
                  pl.BlockSpec((1,) + wq.shape[1:], lambda sl, i, p: (sl, 0, 0)),
                  pl.BlockSpec((1,) + w2pair.shape[1:], lambda sl, i, p: (sl, 0, 0, 0))],
        out_specs=[pl.BlockSpec((1, 1, 1, nrow, LANES), lambda sl, i, p: (sl, i, p, 0, 0)),
                   pl.BlockSpec((1, 8, CMP_HIDDEN), lambda sl, i, p: (sl, 0, 0))],
        out_shape=[jax.ShapeDtypeStruct((2, b, 2, nrow, LANES), BF16),
                   jax.ShapeDtypeStruct((2, 8, CMP_HIDDEN), F32)],
        compiler_params=_params(("arbitrary", "arbitrary", "arbitrary")),
        name="cmp_prompt",
    )(kvc, pe_flat, w1, wq, w2pair)


def _top_blocks(score, n_sel, n_top):
    row = lax.broadcasted_iota(I32, score.shape, 0)
    cnt = jnp.zeros(score.shape, F32)
    for j in range(n_sel):
        sj = score[j:j + 1, :]
        beats = jnp.where(sj > score, 1.0, jnp.where(sj == score, jnp.where(row > j, 1.0, 0.0), 0.0))
        cnt = cnt + beats
    return jnp.where((cnt < n_top) & (row < n_sel), 1.0, 0.0)


def _nsa_prompt_kernel(q_ref, kc_ref, vc_ref, ks_ref, vs_ref, kw_ref, vw_ref, bc_ref, bz_ref, ovt_ref, sp_ref,
                       pc_ref, o_ref, sel_ref, m_ref, l_ref, acc_ref, *, n_cmp, n_sel, n_top):
    tq = ATT_TILE
    tk = ATT_TILE
    cols = HPG * tq
    i = pl.program_id(2)
    q = jnp.concatenate([_dot(q_ref[...], sp_ref[0, h]).astype(BF16) for h in range(HPG)], axis=0)

    def to_tokens(o_t):
        out = jnp.zeros((tq, HPG * HEAD_DIM), F32)
        for h in range(HPG):
            out = out + _dot_tn(o_t[:, h * tq:(h + 1) * tq].astype(BF16), pc_ref[0, h])
        return out.astype(BF16)

    s = _dot_nt(kc_ref[0, 0, 0], q) + bc_ref[0, 0]
    c_io = lax.broadcasted_iota(I32, (LANES, cols), 0)
    l_io = lax.broadcasted_iota(I32, (LANES, cols), 1)
    qpos = i * tq + (l_io & (tq - 1))
    valid = (qpos >= c_io * CMP_STRIDE + (CMP_LEN - 1)) & (c_io < n_cmp)
    s = jnp.where(valid, s, NEG)
    e = jnp.exp(s - jnp.max(s, axis=0, keepdims=True))
    p = jnp.where(valid, e / jnp.sum(e, axis=0, keepdims=True), 0.0)
    o_ref[0] = to_tokens(_dot_tn(vc_ref[0, 0, 0], p.astype(BF16)))

    psum = p[:, 0:tq] + p[:, tq:2 * tq] + p[:, 2 * tq:3 * tq] + p[:, 3 * tq:4 * tq]
    p_hi, p_lo = _hilo(psum)
    n_rows = -(-n_sel // 8) * 8
    imp = (_dot(ovt_ref[...], p_hi) + _dot(ovt_ref[...], p_lo))[0:n_rows]
    j_io = lax.broadcasted_iota(I32, (n_rows, tq), 0)
    qp2 = i * tq + lax.broadcasted_iota(I32, (n_rows, tq), 1)
    qblk = lax.shift_right_logical(qp2, int(math.log2(SEL_LEN)))
    forced = (j_io == 0) | (j_io == qblk) | (j_io == qblk - 1)
    score = jnp.where(forced, FORCE, jnp.where(j_io <= qblk, imp, NEG))
    score = jnp.where(j_io < n_sel, score, NEG_PAD)
    sel_ref[0:n_rows, :] = jnp.where(_top_blocks(score, n_sel, n_top) > 0.5, 0.0, NEG)

    def band_scores(k_ref, k0, nk, off, branch):
        k = k_ref[pl.ds(pl.multiple_of(k0, tq), nk), :].astype(BF16)
        u0 = pl.multiple_of(jnp.maximum(BIAS_ZERO_ROW - off, 0), tq)
        return _dot_nt(k, q) + bz_ref[0, branch, pl.ds(u0, nk), :]

    tkc = 4 * tk
    per_chunk = tkc // SEL_LEN

    def slc_chunk(c):
        k0 = c * tkc
        rows = [jnp.broadcast_to(sel_ref[pl.ds(c * per_chunk + t, 1), :], (SEL_LEN, tq)) for t in range(per_chunk)]
        s = band_scores(ks_ref, k0, tkc, i * tq - k0, 0) + jnp.concatenate([jnp.concatenate(rows, axis=0)] * HPG, axis=1)
        return s, vs_ref[pl.ds(pl.multiple_of(k0, tq), tkc), :].astype(BF16)

    c_diag = lax.shift_right_logical(i, 2)
    s, v = slc_chunk(c_diag)
    m = jnp.max(s, axis=0, keepdims=True)
    pp = jnp.exp(s - m)
    m_ref[...] = jnp.broadcast_to(m, m_ref.shape)
    l_ref[...] = jnp.broadcast_to(jnp.sum(pp, axis=0, keepdims=True), l_ref.shape)
    acc_ref[...] = _dot_tn(v, pp.astype(BF16))

    def earlier_chunk(c, carry):
        s, v = slc_chunk(c)
        m_prev = m_ref[0:1, :]
        m_new = jnp.maximum(m_prev, jnp.max(s, axis=0, keepdims=True))
        alpha = jnp.exp(m_prev - m_new)
        pp = jnp.exp(s - m_new)
        l_ref[...] = jnp.broadcast_to(alpha * l_ref[0:1, :] + jnp.sum(pp, axis=0, keepdims=True), l_ref.shape)
        acc_ref[...] = alpha * acc_ref[...] + _dot_tn(v, pp.astype(BF16))
        m_ref[...] = jnp.broadcast_to(m_new, m_ref.shape)
        return carry

    lax.fori_loop(0, c_diag, earlier_chunk, 0)
    o_ref[1] = to_tokens(acc_ref[...] / l_ref[0:1, :])

    k0 = jnp.maximum(i * tq - WINDOW, 0)
    s = band_scores(kw_ref, k0, WINDOW + tq, i * tq - k0, 1)
    pp = jnp.exp(s - jnp.max(s, axis=0, keepdims=True))
    vw = vw_ref[pl.ds(pl.multiple_of(k0, tq), WINDOW + tq), :].astype(BF16)
    o_ref[2] = to_tokens(_dot_tn(vw, pp.astype(BF16)) / jnp.sum(pp, axis=0, keepdims=True))


def _nsa_prompt(q, kcv, kvc, kvw, b, bias_cmp, btile, ovt):
    s = kcv.shape[3] * CMP_STRIDE
    tq = ATT_TILE
    nq = s // tq
    n_cmp = (s - CMP_LEN) // CMP_STRIDE + 1
    n_sel = -(-s // SEL_LEN)
    n_top = min(SEL_TOP, n_sel)
    assert kcv.shape[3] == LANES and s % tq == 0 and WINDOW % tq == 0
    spread = np.zeros((2, HPG, HPG * HEAD_DIM, LANES), np.float32)
    dd = np.arange(HEAD_DIM)
    for gi in range(2):
        for h in range(HPG):
            spread[gi, h, h * HEAD_DIM + dd, gi * HEAD_DIM + dd] = 1.0
    collect = jnp.asarray(spread.transpose(0, 1, 3, 2), BF16)
    spread = jnp.asarray(spread, BF16)
    cols = HPG * tq
    kv_lane = lambda blk: pl.BlockSpec((s, LANES), lambda bi, gi, i: (bi, blk + gi // 2))
    cspec = lambda slot: pl.BlockSpec((1, 1, 1, LANES, LANES), lambda bi, gi, i: (slot, bi, gi // 2, 0, 0))
    return pl.pallas_call(
        functools.partial(_nsa_prompt_kernel, n_cmp=n_cmp, n_sel=n_sel, n_top=n_top),
        grid=(b, N_KV, nq),
        in_specs=[pl.BlockSpec((tq, HPG * HEAD_DIM), lambda bi, gi, i: (bi * nq + i, gi)),
                  cspec(0), cspec(1), kv_lane(4), kv_lane(6), kv_lane(0), kv_lane(2),
                  pl.BlockSpec((1, 1, LANES, cols), lambda bi, gi, i: (gi, i, 0, 0)),
                  pl.BlockSpec((1,) + btile.shape[1:], lambda bi, gi, i: (gi, 0, 0, 0)),
                  _full(ovt.shape),
                  pl.BlockSpec((1,) + spread.shape[1:], lambda bi, gi, i: (gi % 2, 0, 0, 0)),
                  pl.BlockSpec((1,) + collect.shape[1:], lambda bi, gi, i: (gi % 2, 0, 0, 0))],
        out_specs=pl.BlockSpec((3, tq, HPG * HEAD_DIM), lambda bi, gi, i: (0, bi * nq + i, gi)),
        out_shape=jax.ShapeDtypeStruct((3, b * s, N_HEADS * HEAD_DIM), BF16),
        scratch_shapes=[pltpu.VMEM((LANES, tq), F32), pltpu.VMEM((8, cols), F32), pltpu.VMEM((8, cols), F32),
                        pltpu.VMEM((LANES, cols), F32)],
        compiler_params=_params(("parallel", "parallel", "arbitrary")),
        name="nsa_prompt",
    )(q, kcv, kcv, kvc, kvc, kvw, kvw, bias_cmp, btile, ovt, spread, collect)


def _nsa_sample_kernel(pt_ref, *refs, n_pages, n_sel, n_top):
    pages = refs[:n_pages]
    (q_ref, kvcn_ref, kvwn_ref, swin_ref, wq_ref, pet_ref, w2p_ref, bct_ref, bst_ref, bwt_ref, mwt_ref, mnt_ref,
     ovt_ref, hsum_ref, hm_ref, fold_ref, foldt_ref, selq_ref, o_ref, s_ref, rows_ref) = refs[n_pages:]
    del pt_ref
    gd = N_KV * HEAD_DIM
    nrow = n_pages * (LANES // CMP_STRIDE)
    for k, pg in enumerate(pages):
        for c in range(4):
            rows_ref[c, k * LANES:(k + 1) * LANES, :] = pg[0, 0, c * LANES:(c + 1) * LANES, :].T
    q_rep = jnp.concatenate([q_ref[0].astype(F32)] * N_HEADS, axis=0) * hm_ref[...]
    qr32 = _dot(q_rep.astype(BF16), fold_ref[...])
    qr = qr32.astype(BF16)
    qr_t = qr32.T.astype(BF16)

    def to_tokens(o):
        x = _dot(o.astype(BF16), foldt_ref[...]) * hm_ref[...]
        return _dot(selq_ref[...], x.astype(BF16)).astype(BF16)

    def page_slot_t(k, slot):
        return pages[k][0, 0, slot * gd:(slot + 1) * gd, :].astype(BF16)

    kc_all = []
    for slot in range(2):
        out = jnp.zeros((nrow, gd), F32)
        for pair in range(2):
            chunk = 2 * slot + pair

            def tap(r, chunk=chunk):
                return rows_ref[chunk, pl.ds(r, nrow, stride=CMP_STRIDE), :]

            acc = _half_block_products(tap, wq_ref, slot, nrow)
            for gi in range(2):
                out = out + _block_summaries(acc, gi, pet_ref[slot, 0:1], w2p_ref[slot, 2 * pair + gi], nrow)
        kc_all.append(out.astype(BF16))

    cols = LANES

    def softmax_t(s):
        m = jnp.max(s, axis=0, keepdims=True)
        e = jnp.exp(s - m)
        return e / jnp.sum(e, axis=0, keepdims=True)

    def as_column(row):
        return jnp.broadcast_to(row, (cols, cols)).T[:, 0:1]

    n_cmp = nrow - 1
    c_io = lax.broadcasted_iota(I32, (nrow, cols), 0)
    s = _dot_nt(kc_all[0], qr) + bct_ref[...]
    valid = c_io < n_cmp
    p = jnp.where(valid, softmax_t(jnp.where(valid, s, NEG)), 0.0)
    o_ref[0, 0] = to_tokens(_dot_tn(p.astype(BF16), kc_all[1]))

    pg_sum = _dot_hilo_l(p, hsum_ref[...])
    g_hi, g_lo = _hilo(pg_sum)
    imp = _dot(ovt_ref[...], g_hi) + _dot(ovt_ref[...], g_lo)
    j_io = lax.broadcasted_iota(I32, imp.shape, 0)
    qblk = n_sel - 1
    forced = (j_io == 0) | (j_io == qblk) | (j_io == qblk - 1)
    score = jnp.where(forced, FORCE, jnp.where(j_io <= qblk, imp, NEG))
    score = jnp.where(j_io < n_sel, score, NEG_PAD)
    sel = _top_blocks(score, n_sel, n_top)

    per_page = LANES // SEL_LEN
    for k in range(n_pages):
        sk = _dot_tn(page_slot_t(k, 2), qr_t) + bst_ref[k * LANES:(k + 1) * LANES, :]
        mk = jnp.concatenate([jnp.broadcast_to(sel[per_page * k + t:per_page * k + t + 1, :], (SEL_LEN, cols))
                              for t in range(per_page)], axis=0)
        s_ref[k * LANES:(k + 1) * LANES, :] = jnp.where(mk > 0.5, sk, NEG)
    zpad = jnp.zeros((LANES - 8, gd), F32)
    kn = jnp.concatenate([kvcn_ref[0][:, 2 * gd:3 * gd], zpad], axis=0).astype(BF16)
    vn = jnp.concatenate([kvcn_ref[0][:, 3 * gd:4 * gd], zpad], axis=0).astype(BF16)
    sn = _dot_nt(kn, qr) + bst_ref[n_pages * LANES:(n_pages + 1) * LANES, :]
    mn = (mnt_ref[...] > 0.5) & (jnp.broadcast_to(sel[n_sel - 1:n_sel, :], (LANES, cols)) > 0.5)
    s_ref[n_pages * LANES:(n_pages + 1) * LANES, :] = jnp.where(mn, sn, NEG)
    n_keys = (n_pages + 1) * LANES
    m = jnp.max(s_ref[0:n_keys, :], axis=0, keepdims=True)
    den = jnp.zeros((1, cols), F32)
    o_t = jnp.zeros((gd, cols), F32)
    for k in range(n_pages):
        pk = jnp.exp(s_ref[k * LANES:(k + 1) * LANES, :] - m)
        den = den + jnp.sum(pk, axis=0, keepdims=True)
        o_t = o_t + _dot(page_slot_t(k, 3), pk.astype(BF16))
    pk = jnp.exp(s_ref[n_pages * LANES:(n_pages + 1) * LANES, :] - m)
    den = den + jnp.sum(pk, axis=0, keepdims=True)
    o = o_t.T + _dot_tn(pk.astype(BF16), vn)
    o_ref[1, 0] = to_tokens(o / as_column(den))

    w_buf = swin_ref.shape[3]
    kwn = jnp.concatenate([kvwn_ref[0][:, 0:gd], zpad], axis=0).astype(BF16)
    vwn = jnp.concatenate([kvwn_ref[0][:, gd:2 * gd], zpad], axis=0).astype(BF16)
    sw = jnp.concatenate([_dot_tn(swin_ref[0, 0, 0:gd, :].astype(BF16), qr_t), _dot_nt(kwn, qr)], axis=0) + bwt_ref[...]
    sw = jnp.where(mwt_ref[...] > 0.5, sw, NEG)
    mw = jnp.max(sw, axis=0, keepdims=True)
    pw = jnp.exp(sw - mw)
    denw = jnp.sum(pw, axis=0, keepdims=True)
    ow = (_dot(swin_ref[0, 0, gd:2 * gd, :].astype(BF16), pw[0:w_buf].astype(BF16)).T
          + _dot_tn(pw[w_buf:].astype(BF16), vwn))
    o_ref[2, 0] = to_tokens(ow / as_column(denw))


def _nsa_sample(layer, page_flat, cache4, q_s, kvcn, kvwn, swin, wq, pet, w2p,
                bct, bst, bwt, mwt, mnt, ovt, hsum, n_sel):
    bs, n_new, d = q_s.shape
    r_head = np.arange(LANES) // n_new
    c_head = np.arange(d) // HEAD_DIM
    hm = jnp.asarray((r_head[:, None] == c_head[None]).astype(np.float32))
    fold_np = np.zeros((d, N_KV * HEAD_DIM), np.float32)
    fold_np[np.arange(d), (c_head // HPG) * HEAD_DIM + np.arange(d) % HEAD_DIM] = 1.0
    fold, foldt = jnp.asarray(fold_np, BF16), jnp.asarray(fold_np.T, BF16)
    selq = jnp.asarray((np.arange(n_new)[:, None] == (np.arange(LANES) % n_new)[None]).astype(np.float32), BF16)
    n_pages = page_flat.shape[0] // bs
    w_buf = swin.shape[3]
    n_top = min(SEL_TOP, n_sel)
    page_specs = [pl.BlockSpec((1, 1, cache4.shape[2], LANES),
                               functools.partial(lambda b, pt, k: (layer, pt[b * n_pages + k], 0, 0), k=k))
                  for k in range(n_pages)]
    cfull = lambda a: pl.BlockSpec(a.shape, lambda b, pt: (0,) * a.ndim)
    in_specs = page_specs + [
        pl.BlockSpec((1, n_new, d), lambda b, pt: (b, 0, 0)),
        pl.BlockSpec((1,) + kvcn.shape[1:], lambda b, pt: (b, 0, 0)),
        pl.BlockSpec((1,) + kvwn.shape[1:], lambda b, pt: (b, 0, 0)),
        pl.BlockSpec((1, 1, swin.shape[2], w_buf), lambda b, pt: (layer, b, 0, 0)),
        cfull(wq), cfull(pet), cfull(w2p), cfull(bct), cfull(bst), cfull(bwt), cfull(mwt), cfull(mnt),
        cfull(ovt), cfull(hsum), cfull(hm), cfull(fold), cfull(foldt), cfull(selq)]
    args = [cache4] * n_pages + [q_s, kvcn, kvwn, swin, wq, pet, w2p, bct, bst, bwt, mwt, mnt, ovt, hsum,
                                 hm, fold, foldt, selq]
    grid_spec = pltpu.PrefetchScalarGridSpec(
        num_scalar_prefetch=1, grid=(bs,), in_specs=in_specs,
        out_specs=pl.BlockSpec((3, 1, n_new, d), lambda b, pt: (0, b, 0, 0)),
        scratch_shapes=[pltpu.VMEM(((n_pages + 1) * LANES, LANES), F32),
                        pltpu.VMEM((4, n_pages * LANES, LANES), F32)])
    return pl.pallas_call(
        functools.partial(_nsa_sample_kernel, n_pages=n_pages, n_sel=n_sel, n_top=n_top), grid_spec=grid_spec,
        out_shape=jax.ShapeDtypeStruct((3, bs, n_new, d), BF16),
        compiler_params=_params(("arbitrary",)),
        name="nsa_sample",
    )(page_flat, *args)


def _merge_kernel(x_ref, gu_ref, v_ref, o3p_ref, o3s_ref, gn_ref, gm_ref, wmix_ref, bmix_ref, ex_ref, wb_ref,
                  wo_ref, nf_ref, wrh_ref, wrl_ref, br_ref, x1_o, h2_o, ei_o, rw_o, *, n_prompt_tiles):
    tm = x_ref.shape[0]
    is_prompt = pl.program_id(0) < n_prompt_tiles
    mixed = []
    for c in range(tm // CHUNK):
        vb = v_ref[c * CHUNK:(c + 1) * CHUNK, :].astype(BF16)
        mixed.append(jnp.concatenate(
            [_dot(wmix_ref[0, g], vb[:, g * A_GROUP_WIDTH:(g + 1) * A_GROUP_WIDTH]) for g in range(A_GROUPS)],
            axis=1) + bmix_ref[0])
    o_a = gu_ref[...].astype(F32) * jnp.concatenate(mixed, axis=0)
    gn = gn_ref[...]
    o_b = jnp.zeros((tm, D_MODEL), F32)
    for br in range(3):
        o_br = jnp.where(is_prompt, o3p_ref[br], o3s_ref[br])
        o_b = o_b + _dot_hilo_l(gn, ex_ref[br]) * o_br.astype(F32)
    gm = gm_ref[...].astype(F32)
    merged = (gm[:, :D_MODEL] * _dot(o_a.astype(BF16), wb_ref[0])
              + gm[:, D_MODEL:] * _dot(o_b.astype(BF16), wb_ref[1]))
    x1 = x_ref[...] + _dot(merged.astype(BF16), wo_ref[...])
    x1_o[...] = x1
    h2 = _rms(x1, nf_ref[...])
    h2_o[...] = h2
    hh, hl = _hilo(h2)
    logit = _dot(hh, wrh_ref[...]) + _dot(hl, wrh_ref[...]) + _dot(hh, wrl_ref[...]) + br_ref[...]
    lane = lax.broadcasted_iota(I32, logit.shape, 1)
    big = jnp.int32(9999)
    is_g = lane < N_GROUPS
    gl = jnp.where(is_g, logit, -jnp.inf)
    gmax = jnp.max(gl, axis=-1, keepdims=True)
    grp = jnp.min(jnp.where(gl == gmax, lane, big), axis=-1, keepdims=True)
    p_grp = 1.0 / jnp.sum(jnp.where(is_g, jnp.exp(logit - gmax), 0.0), axis=-1, keepdims=True)
    e_lane = lane - N_GROUPS
    in_grp = (e_lane >= 0) & (lax.shift_right_arithmetic(e_lane, 3) == grp) & (e_lane < N_EXPERTS)
    el = jnp.where(in_grp, logit, -jnp.inf)
    t1 = jnp.max(el, axis=-1, keepdims=True)
    i1 = jnp.min(jnp.where(el == t1, lane, big), axis=-1, keepdims=True)
    el2 = jnp.where(lane == i1, -jnp.inf, el)
    t2 = jnp.max(el2, axis=-1, keepdims=True)
    i2 = jnp.min(jnp.where(el2 == t2, lane, big), axis=-1, keepdims=True)
    r = jnp.exp(t2 - t1)
    w1 = p_grp / (1.0 + r)
    w2 = p_grp * r / (1.0 + r)
    ei_o[...] = jnp.where(lane == 0, i1 - N_GROUPS, jnp.where(lane == 1, i2 - N_GROUPS, 0))
    rw_o[...] = jnp.where(lane == 0, w1, jnp.where(lane == 1, w2, 0.0))


def _merge(x, gu, v, o3p, o3s, gn, gm, wmix, bmix, ex, wb, wo, nf, wrh, wrl, brr, n_prompt_tiles):
    t = x.shape[0]
    tm = TOK_TILE
    row = lambda n: pl.BlockSpec((tm, n), lambda i: (i, 0))
    kind = lambda i: jnp.where(i < n_prompt_tiles, 0, 1)
    outs = [(D_MODEL, F32), (D_MODEL, F32), (LANES, I32), (LANES, F32)]
    return pl.pallas_call(
        functools.partial(_merge_kernel, n_prompt_tiles=n_prompt_tiles),
        grid=(t // tm,),
        in_specs=[row(D_MODEL), row(A_WIDTH), row(A_WIDTH),
                  pl.BlockSpec((3, tm, D_MODEL), lambda i: (0, jnp.minimum(i, n_prompt_tiles - 1), 0)),
                  pl.BlockSpec((3, tm, D_MODEL), lambda i: (0, jnp.maximum(i - n_prompt_tiles, 0), 0)),
                  row(LANES), row(2 * D_MODEL),
                  pl.BlockSpec((1, A_GROUPS, CHUNK, CHUNK), lambda i: (kind(i), 0, 0, 0)),
                  pl.BlockSpec((1, CHUNK, A_WIDTH), lambda i: (kind(i), 0, 0)),
                  _full(ex.shape), _full(wb.shape), _full(wo.shape), _full((1, D_MODEL)),
                  _full(wrh.shape), _full(wrl.shape), _full((1, LANES))],
        out_specs=[row(n) for n, _ in outs],
        out_shape=[jax.ShapeDtypeStruct((t, n), d) for n, d in outs],
        compiler_params=_params(("parallel",)),
        name="merge",
    )(x, gu, v, o3p, o3s, gn, gm, wmix, bmix, ex, wb, wo, nf.reshape(1, -1), wrh, wrl, brr)


def _route1_kernel(ei_ref, ltri_ref, rank_o, cnt_o, carry):
    @pl.when(pl.program_id(0) == 0)
    def _():
        carry[...] = jnp.zeros(carry.shape, F32)

    ei = ei_ref[...]
    lane = lax.broadcasted_iota(I32, ei.shape, 1)
    e1, e2 = ei[:, 0:1], ei[:, 1:2]
    oh = jnp.where((lane == e1) | (lane == e2), 1.0, 0.0)
    cum = _dot(ltri_ref[...], oh.astype(BF16)) + carry[0:1, :]
    r1 = jnp.sum(jnp.where(lane == e1, cum, 0.0), axis=-1, keepdims=True)
    r2 = jnp.sum(jnp.where(lane == e2, cum, 0.0), axis=-1, keepdims=True)
    rank_o[...] = jnp.where(lane == 0, r1, jnp.where(lane == 1, r2, 0.0)).astype(I32)
    carry[...] = carry[...] + jnp.sum(oh, axis=0, keepdims=True)
    cnt_o[...] = carry[...]


def _route2_kernel(ei_ref, rank_ref, cnt_ref, utri_ref, dest_o, be_o, nu_o):
    shift = int(math.log2(MOE_BLOCK))
    nb = lax.shift_right_logical(cnt_ref[...].astype(I32) + (MOE_BLOCK - 1), shift).astype(F32)
    start = _dot(nb.astype(BF16), utri_ref[...])
    ei = ei_ref[...]
    lane = lax.broadcasted_iota(I32, ei.shape, 1)
    e1, e2 = ei[:, 0:1], ei[:, 1:2]
    s1 = jnp.sum(jnp.where(lane == e1, start[0:1, :], 0.0), axis=-1, keepdims=True)
    s2 = jnp.sum(jnp.where(lane == e2, start[0:1, :], 0.0), axis=-1, keepdims=True)
    rk = rank_ref[...]
    d1 = s1.astype(I32) * MOE_BLOCK + rk[:, 0:1]
    d2 = s2.astype(I32) * MOE_BLOCK + rk[:, 1:2]
    dest_o[...] = jnp.where(lane == 0, d1, jnp.where(lane == 1, d2, 0))
    end = start[0:1, :] + nb[0:1, :]
    j = lax.broadcasted_iota(I32, be_o.shape, 0).astype(F32)
    l2 = lax.broadcasted_iota(I32, be_o.shape, 1)
    ge = jnp.sum(jnp.where((l2 < N_EXPERTS) & (end <= j), 1.0, 0.0), axis=-1, keepdims=True)
    be_o[...] = jnp.broadcast_to(jnp.minimum(ge, N_EXPERTS - 1.0).astype(I32), be_o.shape)
    l3 = lax.broadcasted_iota(I32, nu_o.shape, 1)
    nu = jnp.sum(jnp.where(l3 == N_EXPERTS - 1, jnp.broadcast_to(end, nu_o.shape), 0.0), axis=-1, keepdims=True)
    nu_o[...] = jnp.broadcast_to(nu.astype(I32), nu_o.shape)


def _route(ei, n_blocks):
    t = ei.shape[0]
    tm = TOK_TILE
    row = pl.BlockSpec((tm, LANES), lambda i: (i, 0))
    ltri = jnp.asarray(np.tril(np.ones((tm, tm), np.float32), -1), BF16)
    utri = jnp.asarray(np.triu(np.ones((LANES, LANES), np.float32), 1), BF16)
    rank, cnt = pl.pallas_call(
        _route1_kernel, grid=(t // tm,),
        in_specs=[row, _full((tm, tm))],
        out_specs=[row, _full((8, LANES))],
        out_shape=[jax.ShapeDtypeStruct((t, LANES), I32), jax.ShapeDtypeStruct((8, LANES), F32)],
        scratch_shapes=[pltpu.VMEM((8, LANES), F32)],
        compiler_params=_params(("arbitrary",)),
        name="route_rank",
    )(ei, ltri)
    nbp = -(-n_blocks // 8) * 8
    dest, be, nu = pl.pallas_call(
        _route2_kernel, grid=(t // tm,),
        in_specs=[row, row, _full((8, LANES)), _full((LANES, LANES))],
        out_specs=[row, _full((nbp, LANES)), _full((8, LANES))],
        out_shape=[jax.ShapeDtypeStruct((t, LANES), I32), jax.ShapeDtypeStruct((nbp, LANES), I32),
                   jax.ShapeDtypeStruct((8, LANES), I32)],
        compiler_params=_params(("arbitrary",)),
        name="route_slots",
    )(ei, rank, cnt, utri)
    return dest[:, :2].reshape(-1), be[:n_blocks, 0], nu[0, :1]


def _dispatch_kernel(dest_ref, h_ref, xs_in, xs_out, sem, *, chunk):
    del xs_in
    i = pl.program_id(0)

    def row_copy(r, d):
        return pltpu.make_async_copy(h_ref.at[pl.ds(r, 1)], xs_out.at[pl.ds(d, 1)], sem)

    def issue(r, c):
        t = i * chunk + r
        row_copy(r, dest_ref[2 * t]).start()
        row_copy(r, dest_ref[2 * t + 1]).start()
        return c

    lax.fori_loop(0, chunk, issue, 0, unroll=8)
    for _ in range(2):
        pltpu.make_async_copy(h_ref, xs_out.at[pl.ds(0, chunk)], sem).wait()


def _dispatch(dest_flat, h2, cap):
    t, d = h2.shape
    chunk = TOK_TILE
    grid_spec = pltpu.PrefetchScalarGridSpec(
        num_scalar_prefetch=1, grid=(t // chunk,),
        in_specs=[pl.BlockSpec((chunk, d), lambda i, dr: (i, 0)), pl.BlockSpec(memory_space=pl.ANY)],
        out_specs=pl.BlockSpec(memory_space=pl.ANY),
        scratch_shapes=[pltpu.SemaphoreType.DMA(())])
    return pl.pallas_call(
        functools.partial(_dispatch_kernel, chunk=chunk), grid_spec=grid_spec,
        out_shape=jax.ShapeDtypeStruct((cap, d), h2.dtype),
        input_output_aliases={2: 0},
        compiler_params=_params(("arbitrary",)),
        name="moe_dispatch",
    )(dest_flat, h2, jnp.zeros((cap, d), h2.dtype))


def _expert_kernel(be_ref, nu_ref, x_ref, wg_ref, wu_ref, wd_ref, y_ref, wgb, wub, wdb):
    i = pl.program_id(0)

    @pl.when((i == 0) | (be_ref[i] != be_ref[jnp.maximum(i - 1, 0)]))
    def _():
        wgb[...] = wg_ref[0, 0].astype(BF16)
        wub[...] = wu_ref[0, 0].astype(BF16)
        wdb[...] = wd_ref[0, 0].astype(BF16)

    @pl.when(i < nu_ref[0])
    def _():
        xb = x_ref[...].astype(BF16)
        g = _dot(xb, wgb[...])
        u = _dot(xb, wub[...])
        a = (g * _sigmoid(g) * u).astype(BF16)
        y_ref[...] = _dot(a, wdb[...])

    @pl.when(i >= nu_ref[0])
    def _():
        y_ref[...] = jnp.zeros(y_ref.shape, F32)


def _experts(layer, be, nu, xs, wg, wu, wd):
    cap, d = xs.shape
    bm = MOE_BLOCK
    xin = lambda i, be, nu: (jnp.minimum(i, jnp.maximum(nu[0] - 1, 0)), 0)
    wsel = lambda i, be, nu: (layer, be[i], 0, 0)
    grid_spec = pltpu.PrefetchScalarGridSpec(
        num_scalar_prefetch=2, grid=(cap // bm,),
        in_specs=[pl.BlockSpec((bm, d), xin),
                  pl.BlockSpec((1, 1, d, D_EXPERT), wsel), pl.BlockSpec((1, 1, d, D_EXPERT), wsel),
                  pl.BlockSpec((1, 1, D_EXPERT, d), wsel)],
        out_specs=pl.BlockSpec((bm, d), lambda i, be, nu: (i, 0)),
        scratch_shapes=[pltpu.VMEM((d, D_EXPERT), BF16), pltpu.VMEM((d, D_EXPERT), BF16),
                        pltpu.VMEM((D_EXPERT, d), BF16)])
    return pl.pallas_call(
        _expert_kernel, grid_spec=grid_spec,
        out_shape=jax.ShapeDtypeStruct((cap, d), F32),
        compiler_params=_params(("arbitrary",)),
        name="moe_experts",
    )(be, nu, xs, wg, wu, wd)


def _combine_kernel(dest_ref, x_ref, rw_ref, pe_ref, y_hbm, np_ref, wg_ref, wp_ref, fn_ref, o_ref, ybuf, sem,
                    *, final):
    tm = x_ref.shape[0]
    i = pl.program_id(0)
    n = pl.num_programs(0)

    def row_copy(slot, k, r, d):
        return pltpu.make_async_copy(y_hbm.at[pl.ds(d, 1)], ybuf.at[slot, k, pl.ds(r, 1)], sem.at[slot])

    def issue(tile, slot):
        def body(r, c):
            t = tile * tm + r
            row_copy(slot, 0, r, dest_ref[2 * t]).start()
            row_copy(slot, 1, r, dest_ref[2 * t + 1]).start()
            return c
        lax.fori_loop(0, tm, body, 0, unroll=8)

    slot = i & 1

    @pl.when(i == 0)
    def _():
        issue(0, 0)

    @pl.when(i + 1 < n)
    def _():
        issue(i + 1, 1 - slot)

    for k in range(2):
        pltpu.make_async_copy(y_hbm.at[pl.ds(0, tm)], ybuf.at[slot, k], sem.at[slot]).wait()

    rw = rw_ref[...]
    x2 = x_ref[...] + rw[:, 0:1] * ybuf[slot, 0] + rw[:, 1:2] * ybuf[slot, 1]
    gate = _sigmoid(_dot(_rms(x2, np_ref[...]).astype(BF16), wg_ref[...]))
    x3 = x2 + gate * _dot(pe_ref[...].astype(BF16), wp_ref[...])
    o_ref[...] = _rms(x3, fn_ref[...]) if final else x3


def _combine(dest_flat, x1, rw, pemb, y, norm_ple, wg, wp, final_norm, final):
    t, d = x1.shape
    tm = TOK_TILE
    row = lambda n: pl.BlockSpec((tm, n), lambda i, dr: (i, 0))
    cfull = lambda shape: pl.BlockSpec(shape, lambda i, dr: (0,) * len(shape))
    grid_spec = pltpu.PrefetchScalarGridSpec(
        num_scalar_prefetch=1, grid=(t // tm,),
        in_specs=[row(d), row(LANES), row(PLE_DIM), pl.BlockSpec(memory_space=pl.ANY),
                  cfull((1, d)), cfull(wg.shape), cfull(wp.shape), cfull((1, d))],
        out_specs=row(d),
        scratch_shapes=[pltpu.VMEM((2, 2, tm, d), F32), pltpu.SemaphoreType.DMA((2,))])
    return pl.pallas_call(
        functools.partial(_combine_kernel, final=final), grid_spec=grid_spec,
        out_shape=jax.ShapeDtypeStruct((t, d), F32),
        compiler_params=_params(("arbitrary",)),
        name="moe_combine_ple",
    )(dest_flat, x1, rw, pemb, y, norm_ple.reshape(1, -1), wg, wp, final_norm.reshape(1, -1))


def _per_head_lookup(rel_bias, bkt):
    onehot = (jnp.arange(N_BUCKETS, dtype=I32)[:, None] == jnp.asarray(bkt.reshape(1, -1), I32)).astype(F32)
    tab = jnp.dot(rel_bias.T, onehot, precision=lax.Precision.HIGHEST)
    return tab.reshape((N_HEADS,) + bkt.shape)


def _prompt_tables(rel_bias, s):
    tq = ATT_TILE
    nq = s // tq
    n_cmp = (s - CMP_LEN) // CMP_STRIDE + 1
    n_sel = -(-s // SEL_LEN)
    c = np.arange(LANES)
    qpos = np.arange(s).reshape(nq, 1, tq)
    bkt = _bucket_np(qpos - (c * CMP_STRIDE + CMP_LEN - 1)[None, :, None])
    bias_cmp = _per_head_lookup(rel_bias, bkt).reshape(N_KV, HPG, nq, LANES, tq)
    bias_cmp = bias_cmp.transpose(0, 2, 3, 1, 4).reshape(N_KV, nq, LANES, HPG * tq)
    u, ql = np.arange(2 * BIAS_ZERO_ROW)[:, None], np.arange(tq)[None]
    assert (_bucket_np(np.arange(BIAS_ZERO_ROW - 4 * tq + 1, 2 * s)) == N_BUCKETS - 1).all()
    dist = ql - u + BIAS_ZERO_ROW
    btile = _per_head_lookup(rel_bias, _bucket_np(dist)).reshape(N_KV, HPG, 2 * BIAS_ZERO_ROW, tq)
    keep = np.stack([dist >= 0, (dist >= 0) & (dist < WINDOW)])[None, :, None]
    btile = jnp.where(jnp.asarray(keep), btile[:, None], NEG)
    btile = btile.transpose(0, 1, 3, 2, 4).reshape(N_KV, 2, 2 * BIAS_ZERO_ROW, HPG * tq)
    cs, ss = np.arange(n_cmp) * CMP_STRIDE, np.arange(n_sel) * SEL_LEN
    overlap = ((cs[:, None] < ss[None] + SEL_LEN) & (cs[:, None] + CMP_LEN > ss[None])).astype(np.float32)
    ovt = np.zeros((LANES, LANES), np.float32)
    ovt[:n_sel, :n_cmp] = overlap.T
    return bias_cmp, btile, jnp.asarray(ovt, BF16)


def _per_column_lookup(rel_bias, bkt, n_new):
    rows = bkt.shape[0]
    onehot = (jnp.asarray(np.repeat(bkt, N_BUCKETS, axis=1), I32)
              == jnp.asarray(np.tile(np.arange(N_BUCKETS), n_new)[None], I32)).astype(F32)
    spread = (rel_bias[None, :, :, None] * jnp.eye(n_new, dtype=F32)[:, None, None, :])
    spread = spread.reshape(n_new * N_BUCKETS, N_HEADS * n_new)
    return jnp.dot(onehot, spread, precision=lax.Precision.HIGHEST).reshape(rows, N_HEADS * n_new)


def _sample_tables(rel_bias, past, n_new, w_buf):
    cols = np.arange(LANES)
    qpos = past + cols % n_new
    qnew = past + np.arange(n_new)
    n_pages = past // LANES
    n_cmp = (past + n_new - CMP_LEN) // CMP_STRIDE + 1
    n_sel = -(-(past + n_new) // SEL_LEN)
    assert n_cmp == past // CMP_STRIDE - 1 and HPG * n_new * N_KV == LANES
    c = np.arange(past // CMP_STRIDE)
    bct = _per_column_lookup(rel_bias, _bucket_np(qnew[None] - (c * CMP_STRIDE + CMP_LEN - 1)[:, None]), n_new)
    key = np.arange((n_pages + 1) * LANES)
    bst = _per_column_lookup(rel_bias, _bucket_np(qnew[None] - key[:, None]), n_new)
    kw = np.arange(w_buf + LANES)
    kpos = np.where(kw < w_buf, past - w_buf + kw, past + kw - w_buf)
    bwt = _per_column_lookup(rel_bias, _bucket_np(qnew[None] - kpos[:, None]), n_new)
    dw = qpos[None] - kpos[:, None]
    mwt = ((dw >= 0) & (dw < WINDOW) & (kpos[:, None] >= 0) & (kw[:, None] < w_buf + n_new)).astype(np.float32)
    kn = np.arange(LANES)
    mnt = ((kn[:, None] < n_new) & (kn[:, None] <= (cols % n_new)[None])).astype(np.float32)
    cs, ss = np.arange(n_cmp) * CMP_STRIDE, np.arange(n_sel) * SEL_LEN
    overlap = ((cs[:, None] < ss[None] + SEL_LEN) & (cs[:, None] + CMP_LEN > ss[None])).astype(np.float32)
    rows_sel = -(-n_sel // 8) * 8
    ovt = np.zeros((rows_sel, past // CMP_STRIDE), np.float32)
    ovt[:n_sel, :n_cmp] = overlap.T
    same = (cols[:, None] // (HPG * n_new) == cols[None] // (HPG * n_new)) & \
           (cols[:, None] % n_new == cols[None] % n_new)
    return (bct, bst, bwt, jnp.asarray(mwt), jnp.asarray(mnt), jnp.asarray(ovt, BF16),
            jnp.asarray(same.astype(np.float32), BF16), n_sel)


def kernel(x_prompt, x_sample, cache_kv, state_win_kv, page_table, p_prompt, p_sample, rel_bias, norm_mix, w_in, v_norm, w_spatial, b_spatial, pe_cmp, w_phi1, w_phi2, w_branch, w_out, norm_ffn, w_router_group, b_router_group, w_router_expert, b_router_expert, w_exp_gate, w_exp_up, w_exp_down, norm_ple, w_ple_gate, w_ple_proj, final_norm):
    b, s, d = x_prompt.shape
    bs, n_new, _ = x_sample.shape
    depth = w_in.shape[0]
    n_pool, page = cache_kv.shape[1], cache_kv.shape[2]
    past = page_table.shape[1] * page
    w_buf = state_win_kv.shape[2]
    tp, ts = b * s, bs * n_new
    t = tp + ts
    assert page == LANES and tp % TOK_TILE == 0 and ts % TOK_TILE == 0 and n_new == 8 and s >= CHUNK

    x = jnp.concatenate([x_prompt.reshape(tp, d), x_sample.reshape(ts, d)], axis=0)
    pemb = jnp.concatenate([p_prompt.reshape(depth, tp, PLE_DIM), p_sample.reshape(depth, ts, PLE_DIM)], axis=1)
    cache4 = cache_kv.transpose(0, 1, 3, 4, 5, 2).reshape(depth, n_pool, -1, page)
    swin = state_win_kv.transpose(0, 1, 3, 4, 5, 2).reshape(depth, bs, -1, w_buf)
    page_flat = page_table.reshape(-1).astype(I32)

    bias_cmp, btile, ovt_p = _prompt_tables(rel_bias, s)
    bct, bst, bwt, mwt, mnt, ovt_s, hsum, n_sel_s = _sample_tables(rel_bias, past, n_new, w_buf)

    ex = np.zeros((3, LANES, D_MODEL), np.float32)
    for br in range(3):
        ex[br, br * N_HEADS + np.arange(D_MODEL) // HEAD_DIM, np.arange(D_MODEL)] = 1.0
    ex = jnp.asarray(ex, BF16)
    tril = np.tril(np.ones((CHUNK, CHUNK), np.float32))
    blockdiag = np.kron(np.eye(CHUNK // n_new, dtype=np.float32), np.tril(np.ones((n_new, n_new), np.float32)))
    eye_g = jnp.eye(N_KV, dtype=BF16)
    eye_2 = jnp.eye(2, dtype=BF16)

    n_blocks = -(-2 * t // MOE_BLOCK) + N_EXPERTS
    cap = n_blocks * MOE_BLOCK

    kv_p, kv_s, win_p, win_s, v_s = [], [], [], [], []
    for i in range(depth):
        gu, v, q, kvc, kvw, gn, gm = _inproj(x, norm_mix[i], w_in[i], v_norm[i])
        kv_p.append(kvc[:tp].reshape(b, s, 4, N_KV, HEAD_DIM))
        kv_s.append(kvc[tp:].reshape(bs, n_new, 4, N_KV, HEAD_DIM))
        win_p.append(kvw[:tp].reshape(b, s, 2, N_KV, HEAD_DIM)[:, s - min(WINDOW, s):])
        win_s.append(kvw[tp:].reshape(bs, n_new, 2, N_KV, HEAD_DIM))
        v_s.append(v[tp:].reshape(bs, n_new, A_WIDTH))

        w1b = w_phi1[i].astype(BF16)
        w2b = w_phi2[i].astype(BF16)
        pe_flat = jnp.broadcast_to(pe_cmp[i].reshape(2, 1, CMP_LEN * HEAD_DIM), (2, 8, CMP_LEN * HEAD_DIM))
        w1r = w1b.reshape(2, 2, CMP_STRIDE, HEAD_DIM, CMP_HIDDEN)
        wq = jnp.einsum('sprdn,ij->sridjpn', w1r, eye_2).reshape(2, CMP_STRIDE * LANES, 4 * CMP_HIDDEN)
        w2p = jnp.einsum('skd,gh->sgkhd', w2b, eye_g).reshape(2, N_KV, CMP_HIDDEN, N_KV * HEAD_DIM)
        w2pair = jnp.einsum('skd,gh->sgkhd', w2b, eye_2).reshape(2, 2, CMP_HIDDEN, LANES)

        kcv, pet = _cmp_prompt(kvc, b, s, pe_flat.astype(BF16), w1b, wq, w2pair)
        o3_p = _nsa_prompt(q, kcv, kvc, kvw, b, bias_cmp, btile, ovt_p)

        o3_s = _nsa_sample(i, page_flat, cache4, q[tp:].reshape(bs, n_new, d),
                           kvc[tp:].reshape(bs, n_new, -1), kvw[tp:].reshape(bs, n_new, -1), swin,
                           wq, pet, w2p, bct, bst, bwt, mwt, mnt, ovt_s, hsum, n_sel_s).reshape(3, ts, d)

        ws = w_spatial[i]
        wmix = jnp.stack([ws * tril, jnp.tile(ws[:, :n_new, :n_new], (1, CHUNK // n_new, CHUNK // n_new)) * blockdiag])
        bsp = b_spatial[i]
        bmix = jnp.stack([jnp.repeat(bsp.T, A_GROUP_WIDTH, axis=1),
                          jnp.repeat(jnp.tile(bsp[:, :n_new], (1, CHUNK // n_new)).T, A_GROUP_WIDTH, axis=1)])
        wr = jnp.concatenate([w_router_group[i], w_router_expert[i]], axis=1)
        wr = jnp.pad(wr, ((0, 0), (0, LANES - wr.shape[1])))
        wrh = wr.astype(BF16)
        wrl = (wr - wrh.astype(F32)).astype(BF16)
        brr = jnp.pad(jnp.concatenate([b_router_group[i], b_router_expert[i]]), (0, LANES - N_GROUPS - N_EXPERTS))
        x1, h2, ei, rw = _merge(x, gu, v, o3_p, o3_s, gn, gm, wmix.astype(BF16), bmix, ex, w_branch[i].astype(BF16),
                                w_out[i].astype(BF16), norm_ffn[i], wrh, wrl, brr.reshape(1, LANES),
                                tp // TOK_TILE)

        dest_flat, be, nu = _route(ei, n_blocks)
        xs = _dispatch(dest_flat, h2, cap)
        y = _experts(i, be, nu, xs, w_exp_gate, w_exp_up, w_exp_down)
        x = _combine(dest_flat, x1, rw, pemb[i], y, norm_ple[i], w_ple_gate[i].astype(BF16),
                     w_ple_proj[i].astype(BF16), final_norm, final=(i == depth - 1))

    y_prompt = x[:tp].reshape(b, s, d)
    y_sample = x[tp:].reshape(bs, n_new, d)
    win_sample = jnp.concatenate([state_win_kv[:, :, n_new:], jnp.stack(win_s)], axis=2)
    return (y_prompt, y_sample, jnp.stack(kv_p), jnp.stack(kv_s), jnp.stack(win_p), win_sample, jnp.stack(v_s))
```

```python
import functools
import math

import numpy as np
import jax
import jax.numpy as jnp
from jax import lax
from jax.experimental import pallas as pl
from jax.experimental.pallas import tpu as pltpu

F32 = jnp.float32
BF16 = jnp.bfloat16
I32 = jnp.int32

D_MODEL = 1024
A_WIDTH = 1024
A_GROUPS = 4
A_GROUP_WIDTH = A_WIDTH // A_GROUPS
CHUNK = 128
N_HEADS = 16
HEAD_DIM = 64
N_KV = 4
HPG = N_HEADS // N_KV
CMP_LEN = 32
CMP_STRIDE = 16
CMP_HIDDEN = 256
SEL_LEN = 64
SEL_TOP = 16
WINDOW = 512
N_BUCKETS = 32
MAX_DISTANCE = 128
N_GROUPS = 4
EXPERTS_PER_GROUP = 8
N_EXPERTS = 32
D_EXPERT = 512
PLE_DIM = 256
EPS = 1e-6
NEG = -1e30
FORCE = 1e9
LOG2E = 1.4426950408889634
NEG_PAD = -3e38

LANES = 128
TOK_TILE = 256
ATT_TILE = 128
MOE_BLOCK = 256
BIAS_ZERO_ROW = WINDOW + ATT_TILE
VMEM_LIMIT = 56 * 1024 * 1024


def _dot(a, b):
    return jnp.dot(a, b, preferred_element_type=F32)


def _dot_nt(a, b):
    return lax.dot_general(a, b, (((1,), (1,)), ((), ())), preferred_element_type=F32)


def _dot_tn(a, b):
    return lax.dot_general(a, b, (((0,), (0,)), ((), ())), preferred_element_type=F32)


def _hilo(a):
    hi = a.astype(BF16)
    lo = (a - hi.astype(F32)).astype(BF16)
    return hi, lo


def _dot_hilo_l(a, b):
    hi, lo = _hilo(a)
    return _dot(hi, b) + _dot(lo, b)


def _gelu(x):
    return 0.5 * x * (1.0 + jnp.tanh(0.7978845608028654 * (x + 0.044715 * (x * x * x))))


def _sigmoid(x):
    return 1.0 / (1.0 + jnp.exp(-x))


def _rms(x, gain):
    return x * lax.rsqrt(jnp.mean(x * x, axis=-1, keepdims=True) + EPS) * gain


def _full(shape):
    nd = len(shape)
    return pl.BlockSpec(shape, lambda *_: (0,) * nd)


def _params(sem, vmem=VMEM_LIMIT):
    return pltpu.CompilerParams(dimension_semantics=sem, vmem_limit_bytes=vmem)


def _bucket_np(dist):
    n = np.maximum(np.asarray(dist, np.int64), 0)
    max_exact = N_BUCKETS // 2
    nf = np.maximum(n, max_exact).astype(np.float64)
    large = max_exact + (np.log(nf / max_exact) / math.log(MAX_DISTANCE / max_exact)
                         * (N_BUCKETS - max_exact)).astype(np.int64)
    return np.where(n < max_exact, n, np.minimum(large, N_BUCKETS - 1)).astype(np.int32)


def _inproj_kernel(x_ref, g_ref, wu, wv, wq, wkc, wkw, wgn, wgm, vn_ref,
                   gu_o, v_o, q_o, kvc_o, kvw_o, gn_o, gm_o, kvc16_o, kvw16_o):
    x = x_ref[...]
    hb = _rms(x, g_ref[...]).astype(BF16)
    gu_o[...] = _gelu(_dot(hb, wu[...])).astype(BF16)
    v_o[...] = _rms(_gelu(_dot(hb, wv[...])), vn_ref[...])
    q_o[...] = (_dot(hb, wq[...]) * (HEAD_DIM ** -0.5 * LOG2E)).astype(BF16)
    kvc = _dot(hb, wkc[...])
    kvw = _dot(hb, wkw[...])
    kvc_o[...] = kvc
    kvw_o[...] = kvw
    kvc16_o[...] = kvc.astype(BF16)
    kvw16_o[...] = kvw.astype(BF16)
    gn_o[...] = _sigmoid(_dot(hb, wgn[...]))
    gm_o[...] = _sigmoid(_dot(hb, wgm[...])).astype(BF16)


def _inproj(x, gain, w_in, v_gain):
    t = x.shape[0]
    tm = TOK_TILE
    a = A_WIDTH
    c_q, c_kv, c_gn, c_gm = 2 * a, 3 * a, 3 * a + 1536, 3 * a + 1536 + 48
    wb = w_in.astype(BF16)
    wu, wv, wq = wb[:, :a], wb[:, a:2 * a], wb[:, c_q:c_kv]
    wkc, wkw = wb[:, c_kv:c_kv + 1024], wb[:, c_kv + 1024:c_gn]
    wgn = jnp.pad(wb[:, c_gn:c_gm], ((0, 0), (0, LANES - 48)))
    wgm = wb[:, c_gm:]
    row = lambda n: pl.BlockSpec((tm, n), lambda i: (i, 0))
    outs = [(a, BF16), (a, F32), (a, BF16), (1024, F32), (512, F32), (LANES, F32), (2 * D_MODEL, BF16),
            (1024, BF16), (512, BF16)]
    return pl.pallas_call(
        _inproj_kernel,
        grid=(t // tm,),
        in_specs=[row(D_MODEL), _full((1, D_MODEL)), _full(wu.shape), _full(wv.shape), _full(wq.shape),
                  _full(wkc.shape), _full(wkw.shape), _full(wgn.shape), _full(wgm.shape), _full((1, a))],
        out_specs=[row(n) for n, _ in outs],
        out_shape=[jax.ShapeDtypeStruct((t, n), d) for n, d in outs],
        compiler_params=_params(("parallel",)),
        name="inproj",
    )(x, gain.reshape(1, -1), wu, wv, wq, wkc, wkw, wgn, wgm, v_gain.reshape(1, -1))


def _half_block_products(tap, wq_ref, slot, nrow):
    acc = jnp.zeros((nrow, 4 * CMP_HIDDEN), F32)
    for r2 in range(CMP_STRIDE // 2):
        lhs = jnp.concatenate([tap(2 * r2), tap(2 * r2 + 1)], axis=1).astype(BF16)
        acc = acc + _dot(lhs, wq_ref[slot, r2 * 2 * LANES:(r2 + 1) * 2 * LANES, :])
    return acc


def _block_summaries(acc, gi, pe_row, w2, nrow):
    c0 = gi * 2 * CMP_HIDDEN
    pre = acc[:, c0:c0 + CMP_HIDDEN] + pltpu.roll(acc[:, c0 + CMP_HIDDEN:c0 + 2 * CMP_HIDDEN], nrow - 1, 0) + pe_row
    return _dot(_gelu(pre).astype(BF16), w2)


def _cmp_prompt_kernel(kv_ref, pe_ref, w1_ref, wq_ref, w2_ref, o_ref, pt_ref):
    nrow = kv_ref.shape[0] // CMP_STRIDE
    pe_term = _dot(pe_ref[0], w1_ref[0])
    acc = _half_block_products(lambda r: kv_ref[pl.ds(r, nrow, stride=CMP_STRIDE), :], wq_ref, 0, nrow)
    out = jnp.zeros((nrow, LANES), F32)
    for gi in range(2):
        out = out + _block_summaries(acc, gi, pe_term[0:1], w2_ref[0, gi], nrow)
    o_ref[0, 0, 0] = out.astype(BF16)
    pt_ref[0] = pe_term


def _cmp_prompt(kvc, b, s, pe_flat, w1, wq, w2pair):
    nrow = s // CMP_STRIDE
    return pl.pallas_call(
        _cmp_prompt_kernel,
        grid=(2, b, 2),
        in_specs=[pl.BlockSpec((s, LANES), lambda sl, i, p: (i, 2 * sl + p)),
                  pl.BlockSpec((1,) + pe_flat.shape[1:], lambda sl, i, p: (sl, 0, 0)),
                  pl.BlockSpec((1,) + w1.shape[1:], lambda sl, i, p: (sl, 0, 0)),
                  pl.BlockSpec((1,) + wq.shape[1:], lambda sl, i, p: (sl, 0, 0)),
                  pl.BlockSpec((1,) + w2pair.shape[1:], lambda sl, i, p: (sl, 0, 0, 0))],
        out_specs=[pl.BlockSpec((1, 1, 1, nrow, LANES), lambda sl, i, p: (sl, i, p, 0, 0)),
                   pl.BlockSpec((1, 8, CMP_HIDDEN), lambda sl, i, p: (sl, 0, 0))],
        out_shape=[jax.ShapeDtypeStruct((2, b, 2, nrow, LANES), BF16),
                   jax.ShapeDtypeStruct((2, 8, CMP_HIDDEN), F32)],
        compiler_params=_params(("arbitrary", "arbitrary", "arbitrary")),
        name="cmp_prompt",
    )(kvc, pe_flat, w1, wq, w2pair)


def _top_blocks(score, n_sel, n_top):
    row = lax.broadcasted_iota(I32, score.shape, 0)
    cnt = jnp.zeros(score.shape, F32)
    for j in range(n_sel):
        sj = score[j:j + 1, :]
        beats = jnp.where(sj > score, 1.0, jnp.where(sj == score, jnp.where(row > j, 1.0, 0.0), 0.0))
        cnt = cnt + beats
    return jnp.where((cnt < n_top) & (row < n_sel), 1.0, 0.0)


def _nsa_prompt_kernel(q_ref, kc_ref, vc_ref, ks_ref, vs_ref, kw_ref, vw_ref, bc_ref, bz_ref, ovt_ref, sp_ref,
                       pc_ref, o_ref, sel_ref, m_ref, l_ref, acc_ref, *, n_cmp, n_sel, n_top):
    tq = ATT_TILE
    tk = ATT_TILE
    cols = HPG * tq
    i = pl.program_id(2)
    q = jnp.concatenate([_dot(q_ref[...], sp_ref[0, h]).astype(BF16) for h in range(HPG)], axis=0)

    def to_tokens(o_t):
        out = jnp.zeros((tq, HPG * HEAD_DIM), F32)
        for h in range(HPG):
            out = out + _dot_tn(o_t[:, h * tq:(h + 1) * tq].astype(BF16), pc_ref[0, h])
        return out.astype(BF16)

    s = _dot_nt(kc_ref[0, 0, 0], q) + bc_ref[0, 0]
    c_io = lax.broadcasted_iota(I32, (LANES, cols), 0)
    l_io = lax.broadcasted_iota(I32, (LANES, cols), 1)
    qpos = i * tq + (l_io & (tq - 1))
    valid = (qpos >= c_io * CMP_STRIDE + (CMP_LEN - 1)) & (c_io < n_cmp)
    s = jnp.where(valid, s, NEG)
    e = jnp.exp2(s - jnp.max(s, axis=0, keepdims=True))
    p = jnp.where(valid, e / jnp.sum(e, axis=0, keepdims=True), 0.0)
    o_ref[0] = to_tokens(_dot_tn(vc_ref[0, 0, 0], p.astype(BF16)))

    psum = p[:, 0:tq] + p[:, tq:2 * tq] + p[:, 2 * tq:3 * tq] + p[:, 3 * tq:4 * tq]
    p_hi, p_lo = _hilo(psum)
    n_rows = -(-n_sel // 8) * 8
    imp = (_dot(ovt_ref[...], p_hi) + _dot(ovt_ref[...], p_lo))[0:n_rows]
    j_io = lax.broadcasted_iota(I32, (n_rows, tq), 0)
    qp2 = i * tq + lax.broadcasted_iota(I32, (n_rows, tq), 1)
    qblk = lax.shift_right_logical(qp2, int(math.log2(SEL_LEN)))
    forced = (j_io == 0) | (j_io == qblk) | (j_io == qblk - 1)
    score = jnp.where(forced, FORCE, jnp.where(j_io <= qblk, imp, NEG))
    score = jnp.where(j_io < n_sel, score, NEG_PAD)
    sel_ref[0:n_rows, :] = jnp.where(_top_blocks(score, n_sel, n_top) > 0.5, 0.0, NEG)

    def band_scores(k_ref, k0, nk, off, branch):
        k = k_ref[pl.ds(pl.multiple_of(k0, tq), nk), :]
        u0 = pl.multiple_of(jnp.maximum(BIAS_ZERO_ROW - off, 0), tq)
        return _dot_nt(k, q) + bz_ref[0, branch, pl.ds(u0, nk), :]

    tkc = 4 * tk
    per_chunk = tkc // SEL_LEN

    def slc_chunk(c):
        k0 = c * tkc
        rows = [jnp.broadcast_to(sel_ref[pl.ds(c * per_chunk + t, 1), :], (SEL_LEN, tq)) for t in range(per_chunk)]
        s = band_scores(ks_ref, k0, tkc, i * tq - k0, 0) + jnp.concatenate([jnp.concatenate(rows, axis=0)] * HPG, axis=1)
        return s, vs_ref[pl.ds(pl.multiple_of(k0, tq), tkc), :]

    c_diag = lax.shift_right_logical(i, 2)
    s, v = slc_chunk(c_diag)
    m = jnp.max(s, axis=0, keepdims=True)
    pp = jnp.exp2(s - m)
    m_ref[...] = jnp.broadcast_to(m, m_ref.shape)
    l_ref[...] = jnp.broadcast_to(jnp.sum(pp, axis=0, keepdims=True), l_ref.shape)
    acc_ref[...] = _dot_tn(v, pp.astype(BF16))

    def earlier_chunk(c, carry):
        s, v = slc_chunk(c)
        m_prev = m_ref[0:1, :]
        m_new = jnp.maximum(m_prev, jnp.max(s, axis=0, keepdims=True))
        alpha = jnp.exp2(m_prev - m_new)
        pp = jnp.exp2(s - m_new)
        l_ref[...] = jnp.broadcast_to(alpha * l_ref[0:1, :] + jnp.sum(pp, axis=0, keepdims=True), l_ref.shape)
        acc_ref[...] = alpha * acc_ref[...] + _dot_tn(v, pp.astype(BF16))
        m_ref[...] = jnp.broadcast_to(m_new, m_ref.shape)
        return carry

    lax.fori_loop(0, c_diag, earlier_chunk, 0)
    o_ref[1] = to_tokens(acc_ref[...] / l_ref[0:1, :])

    k0 = jnp.maximum(i * tq - WINDOW, 0)
    s = band_scores(kw_ref, k0, WINDOW + tq, i * tq - k0, 1)
    pp = jnp.exp2(s - jnp.max(s, axis=0, keepdims=True))
    vw = vw_ref[pl.ds(pl.multiple_of(k0, tq), WINDOW + tq), :]
    o_ref[2] = to_tokens(_dot_tn(vw, pp.astype(BF16)) / jnp.sum(pp, axis=0, keepdims=True))


def _nsa_prompt(q, kcv, kvc, kvw, b, bias_cmp, btile, ovt):
    s = kcv.shape[3] * CMP_STRIDE
    tq = ATT_TILE
    nq = s // tq
    n_cmp = (s - CMP_LEN) // CMP_STRIDE + 1
    n_sel = -(-s // SEL_LEN)
    n_top = min(SEL_TOP, n_sel)
    assert kcv.shape[3] == LANES and s % tq == 0 and WINDOW % tq == 0
    spread = np.zeros((2, HPG, HPG * HEAD_DIM, LANES), np.float32)
    dd = np.arange(HEAD_DIM)
    for gi in range(2):
        for h in range(HPG):
            spread[gi, h, h * HEAD_DIM + dd, gi * HEAD_DIM + dd] = 1.0
    collect = jnp.asarray(spread.transpose(0, 1, 3, 2), BF16)
    spread = jnp.asarray(spread, BF16)
    cols = HPG * tq
    kv_lane = lambda blk: pl.BlockSpec((s, LANES), lambda bi, gi, i: (bi, blk + gi // 2))
    cspec = lambda slot: pl.BlockSpec((1, 1, 1, LANES, LANES), lambda bi, gi, i: (slot, bi, gi // 2, 0, 0))
    return pl.pallas_call(
        functools.partial(_nsa_prompt_kernel, n_cmp=n_cmp, n_sel=n_sel, n_top=n_top),
        grid=(b, N_KV, nq),
        in_specs=[pl.BlockSpec((tq, HPG * HEAD_DIM), lambda bi, gi, i: (bi * nq + i, gi)),
                  cspec(0), cspec(1), kv_lane(4), kv_lane(6), kv_lane(0), kv_lane(2),
                  pl.BlockSpec((1, 1, LANES, cols), lambda bi, gi, i: (gi, i, 0, 0)),
                  pl.BlockSpec((1,) + btile.shape[1:], lambda bi, gi, i: (gi, 0, 0, 0)),
                  _full(ovt.shape),
                  pl.BlockSpec((1,) + spread.shape[1:], lambda bi, gi, i: (gi % 2, 0, 0, 0)),
                  pl.BlockSpec((1,) + collect.shape[1:], lambda bi, gi, i: (gi % 2, 0, 0, 0))],
        out_specs=pl.BlockSpec((3, tq, HPG * HEAD_DIM), lambda bi, gi, i: (0, bi * nq + i, gi)),
        out_shape=jax.ShapeDtypeStruct((3, b * s, N_HEADS * HEAD_DIM), BF16),
        scratch_shapes=[pltpu.VMEM((LANES, tq), F32), pltpu.VMEM((8, cols), F32), pltpu.VMEM((8, cols), F32),
                        pltpu.VMEM((LANES, cols), F32)],
        compiler_params=_params(("parallel", "parallel", "arbitrary")),
        name="nsa_prompt",
    )(q, kcv, kcv, kvc, kvc, kvw, kvw, bias_cmp, btile, ovt, spread, collect)


def _nsa_sample_kernel(pt_ref, *refs, n_pages, n_sel, n_top):
    pages = refs[:n_pages]
    (q_ref, kvcn_ref, kvwn_ref, swin_ref, wq_ref, pet_ref, w2p_ref, bct_ref, bst_ref, bwt_ref, mwt_ref, mnt_ref,
     ovt_ref, hsum_ref, hm_ref, fold_ref, foldt_ref, selq_ref, _, o_ref, win_ref, s_ref, rows_ref) = refs[n_pages:]
    del pt_ref
    gd = N_KV * HEAD_DIM
    nrow = n_pages * (LANES // CMP_STRIDE)
    for k, pg in enumerate(pages):
        for c in range(4):
            rows_ref[c, k * LANES:(k + 1) * LANES, :] = pg[0, 0, c * LANES:(c + 1) * LANES, :].T
    q_rep = jnp.concatenate([q_ref[0].astype(F32)] * N_HEADS, axis=0) * hm_ref[...]
    qr32 = _dot(q_rep.astype(BF16), fold_ref[...])
    qr = qr32.astype(BF16)
    qr_t = qr32.T.astype(BF16)

    def to_tokens(o):
        x = _dot(o.astype(BF16), foldt_ref[...]) * hm_ref[...]
        return _dot(selq_ref[...], x.astype(BF16)).astype(BF16)

    def page_slot_t(k, slot):
        return pages[k][0, 0, slot * gd:(slot + 1) * gd, :].astype(BF16)

    kc_all = []
    for slot in range(2):
        out = jnp.zeros((nrow, gd), F32)
        for pair in range(2):
            chunk = 2 * slot + pair

            def tap(r, chunk=chunk):
                return rows_ref[chunk, pl.ds(r, nrow, stride=CMP_STRIDE), :]

            acc = _half_block_products(tap, wq_ref, slot, nrow)
            for gi in range(2):
                out = out + _block_summaries(acc, gi, pet_ref[slot, 0:1], w2p_ref[slot, 2 * pair + gi], nrow)
        kc_all.append(out.astype(BF16))

    cols = LANES

    def softmax_t(s):
        m = jnp.max(s, axis=0, keepdims=True)
        e = jnp.exp2(s - m)
        return e / jnp.sum(e, axis=0, keepdims=True)

    def as_column(row):
        return jnp.broadcast_to(row, (cols, cols)).T[:, 0:1]

    n_cmp = nrow - 1
    c_io = lax.broadcasted_iota(I32, (nrow, cols), 0)
    s = _dot_nt(kc_all[0], qr) + bct_ref[...]
    valid = c_io < n_cmp
    p = jnp.where(valid, softmax_t(jnp.where(valid, s, NEG)), 0.0)
    o_ref[0, 0] = to_tokens(_dot_tn(p.astype(BF16), kc_all[1]))

    pg_sum = _dot_hilo_l(p, hsum_ref[...])
    g_hi, g_lo = _hilo(pg_sum)
    imp = _dot(ovt_ref[...], g_hi) + _dot(ovt_ref[...], g_lo)
    j_io = lax.broadcasted_iota(I32, imp.shape, 0)
    qblk = n_sel - 1
    forced = (j_io == 0) | (j_io == qblk) | (j_io == qblk - 1)
    score = jnp.where(forced, FORCE, jnp.where(j_io <= qblk, imp, NEG))
    score = jnp.where(j_io < n_sel, score, NEG_PAD)
    sel = _top_blocks(score, n_sel, n_top)

    per_page = LANES // SEL_LEN
    for k in range(n_pages):
        sk = _dot_tn(page_slot_t(k, 2), qr_t) + bst_ref[k * LANES:(k + 1) * LANES, :]
        mk = jnp.concatenate([jnp.broadcast_to(sel[per_page * k + t:per_page * k + t + 1, :], (SEL_LEN, cols))
                              for t in range(per_page)], axis=0)
        s_ref[k * LANES:(k + 1) * LANES, :] = jnp.where(mk > 0.5, sk, NEG)
    zpad = jnp.zeros((LANES - 8, gd), F32)
    kn = jnp.concatenate([kvcn_ref[0][:, 2 * gd:3 * gd], zpad], axis=0).astype(BF16)
    vn = jnp.concatenate([kvcn_ref[0][:, 3 * gd:4 * gd], zpad], axis=0).astype(BF16)
    sn = _dot_nt(kn, qr) + bst_ref[n_pages * LANES:(n_pages + 1) * LANES, :]
    mn = (mnt_ref[...] > 0.5) & (jnp.broadcast_to(sel[n_sel - 1:n_sel, :], (LANES, cols)) > 0.5)
    s_ref[n_pages * LANES:(n_pages + 1) * LANES, :] = jnp.where(mn, sn, NEG)
    n_keys = (n_pages + 1) * LANES
    m = jnp.max(s_ref[0:n_keys, :], axis=0, keepdims=True)
    den = jnp.zeros((1, cols), F32)
    o_t = jnp.zeros((gd, cols), F32)
    for k in range(n_pages):
        pk = jnp.exp2(s_ref[k * LANES:(k + 1) * LANES, :] - m)
        den = den + jnp.sum(pk, axis=0, keepdims=True)
        o_t = o_t + _dot(page_slot_t(k, 3), pk.astype(BF16))
    pk = jnp.exp2(s_ref[n_pages * LANES:(n_pages + 1) * LANES, :] - m)
    den = den + jnp.sum(pk, axis=0, keepdims=True)
    o = o_t.T + _dot_tn(pk.astype(BF16), vn)
    o_ref[1, 0] = to_tokens(o / as_column(den))

    w_buf = swin_ref.shape[3]
    kwn = jnp.concatenate([kvwn_ref[0][:, 0:gd], zpad], axis=0).astype(BF16)
    vwn = jnp.concatenate([kvwn_ref[0][:, gd:2 * gd], zpad], axis=0).astype(BF16)
    sw = jnp.concatenate([_dot_tn(swin_ref[0, 0, 0:gd, :].astype(BF16), qr_t), _dot_nt(kwn, qr)], axis=0) + bwt_ref[...]
    sw = jnp.where(mwt_ref[...] > 0.5, sw, NEG)
    mw = jnp.max(sw, axis=0, keepdims=True)
    pw = jnp.exp2(sw - mw)
    denw = jnp.sum(pw, axis=0, keepdims=True)
    ow = (_dot(swin_ref[0, 0, gd:2 * gd, :].astype(BF16), pw[0:w_buf].astype(BF16)).T
          + _dot_tn(pw[w_buf:].astype(BF16), vwn))
    o_ref[2, 0] = to_tokens(ow / as_column(denw))

    n_new = kvwn_ref.shape[1]
    shifted = pltpu.roll(swin_ref[0, 0], w_buf - n_new, 1)
    new_t = jnp.concatenate([kvwn_ref[0], jnp.zeros((LANES - n_new, 2 * gd), F32)], axis=0).T
    tail = pltpu.roll(new_t, LANES - n_new, 1)
    lane = lax.broadcasted_iota(I32, (2 * gd, LANES), 1)
    win_ref[0, 0, :, 0:w_buf - LANES] = shifted[:, 0:w_buf - LANES]
    win_ref[0, 0, :, w_buf - LANES:w_buf] = jnp.where(lane >= LANES - n_new, tail, shifted[:, w_buf - LANES:w_buf])


def _nsa_sample(layer, win_prev, page_flat, cache4, q_s, kvcn, kvwn, swin, wq, pet, w2p,
                bct, bst, bwt, mwt, mnt, ovt, hsum, n_sel):
    bs, n_new, d = q_s.shape
    r_head = np.arange(LANES) // n_new
    c_head = np.arange(d) // HEAD_DIM
    hm = jnp.asarray((r_head[:, None] == c_head[None]).astype(np.float32))
    fold_np = np.zeros((d, N_KV * HEAD_DIM), np.float32)
    fold_np[np.arange(d), (c_head // HPG) * HEAD_DIM + np.arange(d) % HEAD_DIM] = 1.0
    fold, foldt = jnp.asarray(fold_np, BF16), jnp.asarray(fold_np.T, BF16)
    selq = jnp.asarray((np.arange(n_new)[:, None] == (np.arange(LANES) % n_new)[None]).astype(np.float32), BF16)
    n_pages = page_flat.shape[0] // bs
    w_buf = swin.shape[3]
    n_top = min(SEL_TOP, n_sel)
    page_specs = [pl.BlockSpec((1, 1, cache4.shape[2], LANES),
                               functools.partial(lambda b, pt, k: (layer, pt[b * n_pages + k], 0, 0), k=k))
                  for k in range(n_pages)]
    cfull = lambda a: pl.BlockSpec(a.shape, lambda b, pt: (0,) * a.ndim)
    in_specs = page_specs + [
        pl.BlockSpec((1, n_new, d), lambda b, pt: (b, 0, 0)),
        pl.BlockSpec((1,) + kvcn.shape[1:], lambda b, pt: (b, 0, 0)),
        pl.BlockSpec((1,) + kvwn.shape[1:], lambda b, pt: (b, 0, 0)),
        pl.BlockSpec((1, 1, swin.shape[2], w_buf), lambda b, pt: (layer, b, 0, 0)),
        cfull(wq), cfull(pet), cfull(w2p), cfull(bct), cfull(bst), cfull(bwt), cfull(mwt), cfull(mnt),
        cfull(ovt), cfull(hsum), cfull(hm), cfull(fold), cfull(foldt), cfull(selq)]
    in_specs.append(pl.BlockSpec(memory_space=pl.ANY))
    args = [cache4] * n_pages + [q_s, kvcn, kvwn, swin, wq, pet, w2p, bct, bst, bwt, mwt, mnt, ovt, hsum,
                                 hm, fold, foldt, selq, win_prev]
    grid_spec = pltpu.PrefetchScalarGridSpec(
        num_scalar_prefetch=1, grid=(bs,), in_specs=in_specs,
        out_specs=[pl.BlockSpec((3, 1, n_new, d), lambda b, pt: (0, b, 0, 0)),
                   pl.BlockSpec((1, 1, swin.shape[2], w_buf), lambda b, pt: (layer, b, 0, 0))],
        scratch_shapes=[pltpu.VMEM(((n_pages + 1) * LANES, LANES), F32),
                        pltpu.VMEM((4, n_pages * LANES, LANES), F32)])
    return pl.pallas_call(
        functools.partial(_nsa_sample_kernel, n_pages=n_pages, n_sel=n_sel, n_top=n_top), grid_spec=grid_spec,
        out_shape=[jax.ShapeDtypeStruct((3, bs, n_new, d), BF16), jax.ShapeDtypeStruct(swin.shape, F32)],
        input_output_aliases={len(args): 1},
        compiler_params=_params(("arbitrary",)),
        name="nsa_sample",
    )(page_flat, *args)


def _merge_kernel(x_ref, gu_ref, v_ref, o3p_ref, o3s_ref, gn_ref, gm_ref, wmix_ref, bmix_ref, ex_ref, wb_ref,
                  wo_ref, nf_ref, wrh_ref, wrl_ref, br_ref, x1_o, h2_o, ei_o, rw_o, *, n_prompt_tiles):
    tm = x_ref.shape[0]
    is_prompt = pl.program_id(0) < n_prompt_tiles
    mixed = []
    for c in range(tm // CHUNK):
        vb = v_ref[c * CHUNK:(c + 1) * CHUNK, :].astype(BF16)
        mixed.append(jnp.concatenate(
            [_dot(wmix_ref[0, g], vb[:, g * A_GROUP_WIDTH:(g + 1) * A_GROUP_WIDTH]) for g in range(A_GROUPS)],
            axis=1) + bmix_ref[0])
    o_a = gu_ref[...].astype(F32) * jnp.concatenate(mixed, axis=0)
    gn = gn_ref[...]
    o_b = jnp.zeros((tm, D_MODEL), F32)
    for br in range(3):
        o_br = jnp.where(is_prompt, o3p_ref[br], o3s_ref[br])
        o_b = o_b + _dot_hilo_l(gn, ex_ref[br]) * o_br.astype(F32)
    gm = gm_ref[...].astype(F32)
    merged = (gm[:, :D_MODEL] * _dot(o_a.astype(BF16), wb_ref[0])
              + gm[:, D_MODEL:] * _dot(o_b.astype(BF16), wb_ref[1]))
    x1 = x_ref[...] + _dot(merged.astype(BF16), wo_ref[...])
    x1_o[...] = x1
    h2 = _rms(x1, nf_ref[...])
    h2_o[...] = h2
    hh, hl = _hilo(h2)
    logit = _dot(hh, wrh_ref[...]) + _dot(hl, wrh_ref[...]) + _dot(hh, wrl_ref[...]) + br_ref[...]
    lane = lax.broadcasted_iota(I32, logit.shape, 1)
    big = jnp.int32(9999)
    is_g = lane < N_GROUPS
    gl = jnp.where(is_g, logit, -jnp.inf)
    gmax = jnp.max(gl, axis=-1, keepdims=True)
    grp = jnp.min(jnp.where(gl == gmax, lane, big), axis=-1, keepdims=True)
    p_grp = 1.0 / jnp.sum(jnp.where(is_g, jnp.exp(logit - gmax), 0.0), axis=-1, keepdims=True)
    e_lane = lane - N_GROUPS
    in_grp = (e_lane >= 0) & (lax.shift_right_arithmetic(e_lane, 3) == grp) & (e_lane < N_EXPERTS)
    el = jnp.where(in_grp, logit, -jnp.inf)
    t1 = jnp.max(el, axis=-1, keepdims=True)
    i1 = jnp.min(jnp.where(el == t1, lane, big), axis=-1, keepdims=True)
    el2 = jnp.where(lane == i1, -jnp.inf, el)
    t2 = jnp.max(el2, axis=-1, keepdims=True)
    i2 = jnp.min(jnp.where(el2 == t2, lane, big), axis=-1, keepdims=True)
    r = jnp.exp(t2 - t1)
    w1 = p_grp / (1.0 + r)
    w2 = p_grp * r / (1.0 + r)
    ei_o[...] = jnp.where(lane == 0, i1 - N_GROUPS, jnp.where(lane == 1, i2 - N_GROUPS, 0))
    rw_o[...] = jnp.where(lane == 0, w1, jnp.where(lane == 1, w2, 0.0))


def _merge(x, gu, v, o3p, o3s, gn, gm, wmix, bmix, ex, wb, wo, nf, wrh, wrl, brr, n_prompt_tiles):
    t = x.shape[0]
    tm = TOK_TILE
    row = lambda n: pl.BlockSpec((tm, n), lambda i: (i, 0))
    kind = lambda i: jnp.where(i < n_prompt_tiles, 0, 1)
    outs = [(D_MODEL, F32), (D_MODEL, F32), (LANES, I32), (LANES, F32)]
    return pl.pallas_call(
        functools.partial(_merge_kernel, n_prompt_tiles=n_prompt_tiles),
        grid=(t // tm,),
        in_specs=[row(D_MODEL), row(A_WIDTH), row(A_WIDTH),
                  pl.BlockSpec((3, tm, D_MODEL), lambda i: (0, jnp.minimum(i, n_prompt_tiles - 1), 0)),
                  pl.BlockSpec((3, tm, D_MODEL), lambda i: (0, jnp.maximum(i - n_prompt_tiles, 0), 0)),
                  row(LANES), row(2 * D_MODEL),
                  pl.BlockSpec((1, A_GROUPS, CHUNK, CHUNK), lambda i: (kind(i), 0, 0, 0)),
                  pl.BlockSpec((1, CHUNK, A_WIDTH), lambda i: (kind(i), 0, 0)),
                  _full(ex.shape), _full(wb.shape), _full(wo.shape), _full((1, D_MODEL)),
                  _full(wrh.shape), _full(wrl.shape), _full((1, LANES))],
        out_specs=[row(n) for n, _ in outs],
        out_shape=[jax.ShapeDtypeStruct((t, n), d) for n, d in outs],
        compiler_params=_params(("parallel",)),
        name="merge",
    )(x, gu, v, o3p, o3s, gn, gm, wmix, bmix, ex, wb, wo, nf.reshape(1, -1), wrh, wrl, brr)


def _route1_kernel(ei_ref, ltri_ref, rank_o, cnt_o, carry):
    @pl.when(pl.program_id(0) == 0)
    def _():
        carry[...] = jnp.zeros(carry.shape, F32)

    ei = ei_ref[...]
    lane = lax.broadcasted_iota(I32, ei.shape, 1)
    e1, e2 = ei[:, 0:1], ei[:, 1:2]
    oh = jnp.where((lane == e1) | (lane == e2), 1.0, 0.0)
    cum = _dot(ltri_ref[...], oh.astype(BF16)) + carry[0:1, :]
    r1 = jnp.sum(jnp.where(lane == e1, cum, 0.0), axis=-1, keepdims=True)
    r2 = jnp.sum(jnp.where(lane == e2, cum, 0.0), axis=-1, keepdims=True)
    rank_o[...] = jnp.where(lane == 0, r1, jnp.where(lane == 1, r2, 0.0)).astype(I32)
    carry[...] = carry[...] + jnp.sum(oh, axis=0, keepdims=True)
    cnt_o[...] = carry[...]


def _route2_kernel(ei_ref, rank_ref, cnt_ref, utri_ref, dest_o, be_o, nu_o):
    shift = int(math.log2(MOE_BLOCK))
    nb = lax.shift_right_logical(cnt_ref[...].astype(I32) + (MOE_BLOCK - 1), shift).astype(F32)
    start = _dot(nb.astype(BF16), utri_ref[...])
    ei = ei_ref[...]
    lane = lax.broadcasted_iota(I32, ei.shape, 1)
    e1, e2 = ei[:, 0:1], ei[:, 1:2]
    s1 = jnp.sum(jnp.where(lane == e1, start[0:1, :], 0.0), axis=-1, keepdims=True)
    s2 = jnp.sum(jnp.where(lane == e2, start[0:1, :], 0.0), axis=-1, keepdims=True)
    rk = rank_ref[...]
    d1 = s1.astype(I32) * MOE_BLOCK + rk[:, 0:1]
    d2 = s2.astype(I32) * MOE_BLOCK + rk[:, 1:2]
    dest_o[...] = jnp.where(lane == 0, d1, jnp.where(lane == 1, d2, 0))
    end = start[0:1, :] + nb[0:1, :]
    j = lax.broadcasted_iota(I32, be_o.shape, 0).astype(F32)
    l2 = lax.broadcasted_iota(I32, be_o.shape, 1)
    ge = jnp.sum(jnp.where((l2 < N_EXPERTS) & (end <= j), 1.0, 0.0), axis=-1, keepdims=True)
    be_o[...] = jnp.broadcast_to(jnp.minimum(ge, N_EXPERTS - 1.0).astype(I32), be_o.shape)
    l3 = lax.broadcasted_iota(I32, nu_o.shape, 1)
    nu = jnp.sum(jnp.where(l3 == N_EXPERTS - 1, jnp.broadcast_to(end, nu_o.shape), 0.0), axis=-1, keepdims=True)
    nu_o[...] = jnp.broadcast_to(nu.astype(I32), nu_o.shape)


def _route(ei, n_blocks):
    t = ei.shape[0]
    tm = TOK_TILE
    row = pl.BlockSpec((tm, LANES), lambda i: (i, 0))
    ltri = jnp.asarray(np.tril(np.ones((tm, tm), np.float32), -1), BF16)
    utri = jnp.asarray(np.triu(np.ones((LANES, LANES), np.float32), 1), BF16)
    rank, cnt = pl.pallas_call(
        _route1_kernel, grid=(t // tm,),
        in_specs=[row, _full((tm, tm))],
        out_specs=[row, _full((8, LANES))],
        out_shape=[jax.ShapeDtypeStruct((t, LANES), I32), jax.ShapeDtypeStruct((8, LANES), F32)],
        scratch_shapes=[pltpu.VMEM((8, LANES), F32)],
        compiler_params=_params(("arbitrary",)),
        name="route_rank",
    )(ei, ltri)
    nbp = -(-n_blocks // 8) * 8
    dest, be, nu = pl.pallas_call(
        _route2_kernel, grid=(t // tm,),
        in_specs=[row, row, _full((8, LANES)), _full((LANES, LANES))],
        out_specs=[row, _full((nbp, LANES)), _full((8, LANES))],
        out_shape=[jax.ShapeDtypeStruct((t, LANES), I32), jax.ShapeDtypeStruct((nbp, LANES), I32),
                   jax.ShapeDtypeStruct((8, LANES), I32)],
        compiler_params=_params(("arbitrary",)),
        name="route_slots",
    )(ei, rank, cnt, utri)
    return dest[:, :2].reshape(-1), be[:n_blocks, 0], nu[0, :1]


def _dispatch_kernel(dest_ref, h_ref, xs_in, xs_out, sem, *, chunk):
    del xs_in
    i = pl.program_id(0)

    def row_copy(r, d):
        return pltpu.make_async_copy(h_ref.at[pl.ds(r, 1)], xs_out.at[pl.ds(d, 1)], sem)

    def issue(r, c):
        t = i * chunk + r
        row_copy(r, dest_ref[2 * t]).start()
        row_copy(r, dest_ref[2 * t + 1]).start()
        return c

    lax.fori_loop(0, chunk, issue, 0, unroll=8)
    for _ in range(2):
        pltpu.make_async_copy(h_ref, xs_out.at[pl.ds(0, chunk)], sem).wait()


def _dispatch(dest_flat, h2, cap):
    t, d = h2.shape
    chunk = TOK_TILE
    grid_spec = pltpu.PrefetchScalarGridSpec(
        num_scalar_prefetch=1, grid=(t // chunk,),
        in_specs=[pl.BlockSpec((chunk, d), lambda i, dr: (i, 0)), pl.BlockSpec(memory_space=pl.ANY)],
        out_specs=pl.BlockSpec(memory_space=pl.ANY),
        scratch_shapes=[pltpu.SemaphoreType.DMA(())])
    return pl.pallas_call(
        functools.partial(_dispatch_kernel, chunk=chunk), grid_spec=grid_spec,
        out_shape=jax.ShapeDtypeStruct((cap, d), h2.dtype),
        input_output_aliases={2: 0},
        compiler_params=_params(("arbitrary",)),
        name="moe_dispatch",
    )(dest_flat, h2, jnp.zeros((cap, d), h2.dtype))


def _expert_kernel(be_ref, nu_ref, x_ref, wg_ref, wu_ref, wd_ref, y_ref, wgb, wub, wdb):
    i = pl.program_id(0)

    @pl.when((i == 0) | (be_ref[i] != be_ref[jnp.maximum(i - 1, 0)]))
    def _():
        wgb[...] = wg_ref[0, 0].astype(BF16)
        wub[...] = wu_ref[0, 0].astype(BF16)
        wdb[...] = wd_ref[0, 0].astype(BF16)

    @pl.when(i < nu_ref[0])
    def _():
        xb = x_ref[...].astype(BF16)
        g = _dot(xb, wgb[...])
        u = _dot(xb, wub[...])
        a = (g * _sigmoid(g) * u).astype(BF16)
        y_ref[...] = _dot(a, wdb[...])

    @pl.when(i >= nu_ref[0])
    def _():
        y_ref[...] = jnp.zeros(y_ref.shape, F32)


def _experts(layer, be, nu, xs, wg, wu, wd):
    cap, d = xs.shape
    bm = MOE_BLOCK
    xin = lambda i, be, nu: (jnp.minimum(i, jnp.maximum(nu[0] - 1, 0)), 0)
    wsel = lambda i, be, nu: (layer, be[i], 0, 0)
    grid_spec = pltpu.PrefetchScalarGridSpec(
        num_scalar_prefetch=2, grid=(cap // bm,),
        in_specs=[pl.BlockSpec((bm, d), xin),
                  pl.BlockSpec((1, 1, d, D_EXPERT), wsel), pl.BlockSpec((1, 1, d, D_EXPERT), wsel),
                  pl.BlockSpec((1, 1, D_EXPERT, d), wsel)],
        out_specs=pl.BlockSpec((bm, d), lambda i, be, nu: (i, 0)),
        scratch_shapes=[pltpu.VMEM((d, D_EXPERT), BF16), pltpu.VMEM((d, D_EXPERT), BF16),
                        pltpu.VMEM((D_EXPERT, d), BF16)])
    return pl.pallas_call(
        _expert_kernel, grid_spec=grid_spec,
        out_shape=jax.ShapeDtypeStruct((cap, d), F32),
        compiler_params=_params(("arbitrary",)),
        name="moe_experts",
    )(be, nu, xs, wg, wu, wd)


def _combine_kernel(dest_ref, x_ref, rw_ref, pe_ref, y_hbm, np_ref, wg_ref, wp_ref, fn_ref, o_ref, ybuf, sem,
                    *, final):
    tm = x_ref.shape[0]
    i = pl.program_id(0)
    n = pl.num_programs(0)

    def row_copy(slot, k, r, d):
        return pltpu.make_async_copy(y_hbm.at[pl.ds(d, 1)], ybuf.at[slot, k, pl.ds(r, 1)], sem.at[slot])

    def issue(tile, slot):
        def body(r, c):
            t = tile * tm + r
            row_copy(slot, 0, r, dest_ref[2 * t]).start()
            row_copy(slot, 1, r, dest_ref[2 * t + 1]).start()
            return c
        lax.fori_loop(0, tm, body, 0, unroll=8)

    slot = i & 1

    @pl.when(i == 0)
    def _():
        issue(0, 0)

    @pl.when(i + 1 < n)
    def _():
        issue(i + 1, 1 - slot)

    for k in range(2):
        pltpu.make_async_copy(y_hbm.at[pl.ds(0, tm)], ybuf.at[slot, k], sem.at[slot]).wait()

    rw = rw_ref[...]
    x2 = x_ref[...] + rw[:, 0:1] * ybuf[slot, 0] + rw[:, 1:2] * ybuf[slot, 1]
    gate = _sigmoid(_dot(_rms(x2, np_ref[...]).astype(BF16), wg_ref[...]))
    x3 = x2 + gate * _dot(pe_ref[...].astype(BF16), wp_ref[...])
    o_ref[...] = _rms(x3, fn_ref[...]) if final else x3


def _combine(dest_flat, x1, rw, pemb, y, norm_ple, wg, wp, final_norm, final):
    t, d = x1.shape
    tm = TOK_TILE
    row = lambda n: pl.BlockSpec((tm, n), lambda i, dr: (i, 0))
    cfull = lambda shape: pl.BlockSpec(shape, lambda i, dr: (0,) * len(shape))
    grid_spec = pltpu.PrefetchScalarGridSpec(
        num_scalar_prefetch=1, grid=(t // tm,),
        in_specs=[row(d), row(LANES), row(PLE_DIM), pl.BlockSpec(memory_space=pl.ANY),
                  cfull((1, d)), cfull(wg.shape), cfull(wp.shape), cfull((1, d))],
        out_specs=row(d),
        scratch_shapes=[pltpu.VMEM((2, 2, tm, d), F32), pltpu.SemaphoreType.DMA((2,))])
    return pl.pallas_call(
        functools.partial(_combine_kernel, final=final), grid_spec=grid_spec,
        out_shape=jax.ShapeDtypeStruct((t, d), F32),
        compiler_params=_params(("arbitrary",)),
        name="moe_combine_ple",
    )(dest_flat, x1, rw, pemb, y, norm_ple.reshape(1, -1), wg, wp, final_norm.reshape(1, -1))


def _per_head_lookup(rel_bias, bkt):
    onehot = (jnp.arange(N_BUCKETS, dtype=I32)[:, None] == jnp.asarray(bkt.reshape(1, -1), I32)).astype(F32)
    tab = jnp.dot(rel_bias.T, onehot, precision=lax.Precision.HIGHEST)
    return tab.reshape((N_HEADS,) + bkt.shape)


def _prompt_tables(rel_bias, s):
    tq = ATT_TILE
    nq = s // tq
    n_cmp = (s - CMP_LEN) // CMP_STRIDE + 1
    n_sel = -(-s // SEL_LEN)
    c = np.arange(LANES)
    qpos = np.arange(s).reshape(nq, 1, tq)
    bkt = _bucket_np(qpos - (c * CMP_STRIDE + CMP_LEN - 1)[None, :, None])
    bias_cmp = _per_head_lookup(rel_bias, bkt).reshape(N_KV, HPG, nq, LANES, tq)
    bias_cmp = bias_cmp.transpose(0, 2, 3, 1, 4).reshape(N_KV, nq, LANES, HPG * tq)
    u, ql = np.arange(2 * BIAS_ZERO_ROW)[:, None], np.arange(tq)[None]
    assert (_bucket_np(np.arange(BIAS_ZERO_ROW - 4 * tq + 1, 2 * s)) == N_BUCKETS - 1).all()
    dist = ql - u + BIAS_ZERO_ROW
    btile = _per_head_lookup(rel_bias, _bucket_np(dist)).reshape(N_KV, HPG, 2 * BIAS_ZERO_ROW, tq)
    keep = np.stack([dist >= 0, (dist >= 0) & (dist < WINDOW)])[None, :, None]
    btile = jnp.where(jnp.asarray(keep), btile[:, None], NEG)
    btile = btile.transpose(0, 1, 3, 2, 4).reshape(N_KV, 2, 2 * BIAS_ZERO_ROW, HPG * tq)
    cs, ss = np.arange(n_cmp) * CMP_STRIDE, np.arange(n_sel) * SEL_LEN
    overlap = ((cs[:, None] < ss[None] + SEL_LEN) & (cs[:, None] + CMP_LEN > ss[None])).astype(np.float32)
    ovt = np.zeros((LANES, LANES), np.float32)
    ovt[:n_sel, :n_cmp] = overlap.T
    return bias_cmp, btile, jnp.asarray(ovt, BF16)


def _per_column_lookup(rel_bias, bkt, n_new):
    rows = bkt.shape[0]
    onehot = (jnp.asarray(np.repeat(bkt, N_BUCKETS, axis=1), I32)
              == jnp.asarray(np.tile(np.arange(N_BUCKETS), n_new)[None], I32)).astype(F32)
    spread = (rel_bias[None, :, :, None] * jnp.eye(n_new, dtype=F32)[:, None, None, :])
    spread = spread.reshape(n_new * N_BUCKETS, N_HEADS * n_new)
    return jnp.dot(onehot, spread, precision=lax.Precision.HIGHEST).reshape(rows, N_HEADS * n_new)


def _sample_tables(rel_bias, past, n_new, w_buf):
    cols = np.arange(LANES)
    qpos = past + cols % n_new
    qnew = past + np.arange(n_new)
    n_pages = past // LANES
    n_cmp = (past + n_new - CMP_LEN) // CMP_STRIDE + 1
    n_sel = -(-(past + n_new) // SEL_LEN)
    assert n_cmp == past // CMP_STRIDE - 1 and HPG * n_new * N_KV == LANES
    c = np.arange(past // CMP_STRIDE)
    bct = _per_column_lookup(rel_bias, _bucket_np(qnew[None] - (c * CMP_STRIDE + CMP_LEN - 1)[:, None]), n_new)
    key = np.arange((n_pages + 1) * LANES)
    bst = _per_column_lookup(rel_bias, _bucket_np(qnew[None] - key[:, None]), n_new)
    kw = np.arange(w_buf + LANES)
    kpos = np.where(kw < w_buf, past - w_buf + kw, past + kw - w_buf)
    bwt = _per_column_lookup(rel_bias, _bucket_np(qnew[None] - kpos[:, None]), n_new)
    dw = qpos[None] - kpos[:, None]
    mwt = ((dw >= 0) & (dw < WINDOW) & (kpos[:, None] >= 0) & (kw[:, None] < w_buf + n_new)).astype(np.float32)
    kn = np.arange(LANES)
    mnt = ((kn[:, None] < n_new) & (kn[:, None] <= (cols % n_new)[None])).astype(np.float32)
    cs, ss = np.arange(n_cmp) * CMP_STRIDE, np.arange(n_sel) * SEL_LEN
    overlap = ((cs[:, None] < ss[None] + SEL_LEN) & (cs[:, None] + CMP_LEN > ss[None])).astype(np.float32)
    rows_sel = -(-n_sel // 8) * 8
    ovt = np.zeros((rows_sel, past // CMP_STRIDE), np.float32)
    ovt[:n_sel, :n_cmp] = overlap.T
    same = (cols[:, None] // (HPG * n_new) == cols[None] // (HPG * n_new)) & \
           (cols[:, None] % n_new == cols[None] % n_new)
    return (bct, bst, bwt, jnp.asarray(mwt), jnp.asarray(mnt), jnp.asarray(ovt, BF16),
            jnp.asarray(same.astype(np.float32), BF16), n_sel)


def kernel(x_prompt, x_sample, cache_kv, state_win_kv, page_table, p_prompt, p_sample, rel_bias, norm_mix, w_in, v_norm, w_spatial, b_spatial, pe_cmp, w_phi1, w_phi2, w_branch, w_out, norm_ffn, w_router_group, b_router_group, w_router_expert, b_router_expert, w_exp_gate, w_exp_up, w_exp_down, norm_ple, w_ple_gate, w_ple_proj, final_norm):
    b, s, d = x_prompt.shape
    bs, n_new, _ = x_sample.shape
    depth = w_in.shape[0]
    n_pool, page = cache_kv.shape[1], cache_kv.shape[2]
    past = page_table.shape[1] * page
    w_buf = state_win_kv.shape[2]
    tp, ts = b * s, bs * n_new
    t = tp + ts
    assert page == LANES and tp % TOK_TILE == 0 and ts % TOK_TILE == 0 and n_new == 8 and s >= CHUNK

    x = jnp.concatenate([x_prompt.reshape(tp, d), x_sample.reshape(ts, d)], axis=0)
    pemb = jnp.concatenate([p_prompt.reshape(depth, tp, PLE_DIM), p_sample.reshape(depth, ts, PLE_DIM)], axis=1)
    cache4 = cache_kv.transpose(0, 1, 3, 4, 5, 2).reshape(depth, n_pool, -1, page)
    swin = state_win_kv.transpose(0, 1, 3, 4, 5, 2).reshape(depth, bs, -1, w_buf)
    page_flat = page_table.reshape(-1).astype(I32)

    rel_bias2 = rel_bias * LOG2E
    bias_cmp, btile, ovt_p = _prompt_tables(rel_bias2, s)
    bct, bst, bwt, mwt, mnt, ovt_s, hsum, n_sel_s = _sample_tables(rel_bias2, past, n_new, w_buf)

    ex = np.zeros((3, LANES, D_MODEL), np.float32)
    for br in range(3):
        ex[br, br * N_HEADS + np.arange(D_MODEL) // HEAD_DIM, np.arange(D_MODEL)] = 1.0
    ex = jnp.asarray(ex, BF16)
    tril = np.tril(np.ones((CHUNK, CHUNK), np.float32))
    blockdiag = np.kron(np.eye(CHUNK // n_new, dtype=np.float32), np.tril(np.ones((n_new, n_new), np.float32)))
    eye_g = jnp.eye(N_KV, dtype=BF16)
    eye_2 = jnp.eye(2, dtype=BF16)

    n_blocks = -(-2 * t // MOE_BLOCK) + N_EXPERTS
    cap = n_blocks * MOE_BLOCK

    kv_p, kv_s, win_p, v_s = [], [], [], []
    win_s = jnp.zeros(swin.shape, F32)
    for i in range(depth):
        gu, v, q, kvc, kvw, gn, gm, kvc16, kvw16 = _inproj(x, norm_mix[i], w_in[i], v_norm[i])
        kv_p.append(kvc[:tp].reshape(b, s, 4, N_KV, HEAD_DIM))
        kv_s.append(kvc[tp:].reshape(bs, n_new, 4, N_KV, HEAD_DIM))
        win_p.append(kvw[:tp].reshape(b, s, 2, N_KV, HEAD_DIM)[:, s - min(WINDOW, s):])
        v_s.append(v[tp:].reshape(bs, n_new, A_WIDTH))

        w1b = w_phi1[i].astype(BF16)
        w2b = w_phi2[i].astype(BF16)
        pe_flat = jnp.broadcast_to(pe_cmp[i].reshape(2, 1, CMP_LEN * HEAD_DIM), (2, 8, CMP_LEN * HEAD_DIM))
        w1r = w1b.reshape(2, 2, CMP_STRIDE, HEAD_DIM, CMP_HIDDEN)
        wq = jnp.einsum('sprdn,ij->sridjpn', w1r, eye_2).reshape(2, CMP_STRIDE * LANES, 4 * CMP_HIDDEN)
        w2p = jnp.einsum('skd,gh->sgkhd', w2b, eye_g).reshape(2, N_KV, CMP_HIDDEN, N_KV * HEAD_DIM)
        w2pair = jnp.einsum('skd,gh->sgkhd', w2b, eye_2).reshape(2, 2, CMP_HIDDEN, LANES)

        kcv, pet = _cmp_prompt(kvc, b, s, pe_flat.astype(BF16), w1b, wq, w2pair)
        o3_p = _nsa_prompt(q, kcv, kvc16, kvw16, b, bias_cmp, btile, ovt_p)

        o3_s, win_s = _nsa_sample(i, win_s, page_flat, cache4, q[tp:].reshape(bs, n_new, d),
                                  kvc[tp:].reshape(bs, n_new, -1), kvw[tp:].reshape(bs, n_new, -1), swin,
                                  wq, pet, w2p, bct, bst, bwt, mwt, mnt, ovt_s, hsum, n_sel_s)
        o3_s = o3_s.reshape(3, ts, d)

        ws = w_spatial[i]
        wmix = jnp.stack([ws * tril, jnp.tile(ws[:, :n_new, :n_new], (1, CHUNK // n_new, CHUNK // n_new)) * blockdiag])
        bsp = b_spatial[i]
        bmix = jnp.stack([jnp.repeat(bsp.T, A_GROUP_WIDTH, axis=1),
                          jnp.repeat(jnp.tile(bsp[:, :n_new], (1, CHUNK // n_new)).T, A_GROUP_WIDTH, axis=1)])
        wr = jnp.concatenate([w_router_group[i], w_router_expert[i]], axis=1)
        wr = jnp.pad(wr, ((0, 0), (0, LANES - wr.shape[1])))
        wrh = wr.astype(BF16)
        wrl = (wr - wrh.astype(F32)).astype(BF16)
        brr = jnp.pad(jnp.concatenate([b_router_group[i], b_router_expert[i]]), (0, LANES - N_GROUPS - N_EXPERTS))
        x1, h2, ei, rw = _merge(x, gu, v, o3_p, o3_s, gn, gm, wmix.astype(BF16), bmix, ex, w_branch[i].astype(BF16),
                                w_out[i].astype(BF16), norm_ffn[i], wrh, wrl, brr.reshape(1, LANES),
                                tp // TOK_TILE)

        dest_flat, be, nu = _route(ei, n_blocks)
        xs = _dispatch(dest_flat, h2, cap)
        y = _experts(i, be, nu, xs, w_exp_gate, w_exp_up, w_exp_down)
        x = _combine(dest_flat, x1, rw, pemb[i], y, norm_ple[i], w_ple_gate[i].astype(BF16),
                     w_ple_proj[i].astype(BF16), final_norm, final=(i == depth - 1))

    y_prompt = x[:tp].reshape(b, s, d)
    y_sample = x[tp:].reshape(bs, n_new, d)
    win_sample = win_s.reshape(depth, bs, 2, N_KV, HEAD_DIM, w_buf).transpose(0, 1, 5, 2, 3, 4)
    return (y_prompt, y_sample, jnp.stack(kv_p), jnp.stack(kv_s), jnp.stack(win_p), win_sample, jnp.stack(v_s))
```

```python
import functools
import math

import numpy as np
import jax
import jax.numpy as jnp
from jax import lax
from jax.experimental import pallas as pl
from jax.experimental.pallas import tpu as pltpu

F32 = jnp.float32
BF16 = jnp.bfloat16
I32 = jnp.int32

D_MODEL = 1024
A_WIDTH = 1024
A_GROUPS = 4
A_GROUP_WIDTH = A_WIDTH // A_GROUPS
CHUNK = 128
N_HEADS = 16
HEAD_DIM = 64
N_KV = 4
HPG = N_HEADS // N_KV
CMP_LEN = 32
CMP_STRIDE = 16
CMP_HIDDEN = 256
SEL_LEN = 64
SEL_TOP = 16
WINDOW = 512
N_BUCKETS = 32
MAX_DISTANCE = 128
N_GROUPS = 4
EXPERTS_PER_GROUP = 8
N_EXPERTS = 32
D_EXPERT = 512
PLE_DIM = 256
EPS = 1e-6
NEG = -1e30
FORCE = 1e9
LOG2E = 1.4426950408889634
NEG_PAD = -3e38

LANES = 128
TOK_TILE = 256
ATT_TILE = 128
MOE_BLOCK = 256
BIAS_ZERO_ROW = WINDOW + ATT_TILE
VMEM_LIMIT = 56 * 1024 * 1024


def _dot(a, b):
    return jnp.dot(a, b, preferred_element_type=F32)


def _dot_nt(a, b):
    return lax.dot_general(a, b, (((1,), (1,)), ((), ())), preferred_element_type=F32)


def _dot_tn(a, b):
    return lax.dot_general(a, b, (((0,), (0,)), ((), ())), preferred_element_type=F32)


def _hilo(a):
    hi = a.astype(BF16)
    lo = (a - hi.astype(F32)).astype(BF16)
    return hi, lo


def _dot_hilo_l(a, b):
    hi, lo = _hilo(a)
    return _dot(hi, b) + _dot(lo, b)


def _gelu(x):
    return 0.5 * x * (1.0 + jnp.tanh(0.7978845608028654 * (x + 0.044715 * (x * x * x))))


def _sigmoid(x):
    return 1.0 / (1.0 + jnp.exp(-x))


def _rms(x, gain):
    return x * lax.rsqrt(jnp.mean(x * x, axis=-1, keepdims=True) + EPS) * gain


def _full(shape):
    nd = len(shape)
    return pl.BlockSpec(shape, lambda *_: (0,) * nd)


def _params(sem, vmem=VMEM_LIMIT):
    return pltpu.CompilerParams(dimension_semantics=sem, vmem_limit_bytes=vmem)


def _bucket_np(dist):
    n = np.maximum(np.asarray(dist, np.int64), 0)
    max_exact = N_BUCKETS // 2
    nf = np.maximum(n, max_exact).astype(np.float64)
    large = max_exact + (np.log(nf / max_exact) / math.log(MAX_DISTANCE / max_exact)
                         * (N_BUCKETS - max_exact)).astype(np.int64)
    return np.where(n < max_exact, n, np.minimum(large, N_BUCKETS - 1)).astype(np.int32)


def _inproj_kernel(x_ref, g_ref, wu, wv, wq, wkc, wkw, wgn, wgm, vn_ref,
                   gu_o, v_o, q_o, kvc_o, kvw_o, gn_o, gm_o, kvc16_o, kvw16_o):
    x = x_ref[...]
    hb = _rms(x, g_ref[...]).astype(BF16)
    gu_o[...] = _gelu(_dot(hb, wu[...])).astype(BF16)
    v_o[...] = _rms(_gelu(_dot(hb, wv[...])), vn_ref[...])
    q_o[...] = (_dot(hb, wq[...]) * (HEAD_DIM ** -0.5 * LOG2E)).astype(BF16)
    kvc = _dot(hb, wkc[...])
    kvw = _dot(hb, wkw[...])
    kvc_o[...] = kvc
    kvw_o[...] = kvw
    kvc16_o[...] = kvc.astype(BF16)
    kvw16_o[...] = kvw.astype(BF16)
    gn_o[...] = _sigmoid(_dot(hb, wgn[...]))
    gm_o[...] = _sigmoid(_dot(hb, wgm[...])).astype(BF16)


def _inproj(x, gain, w_in, v_gain):
    t = x.shape[0]
    tm = TOK_TILE
    a = A_WIDTH
    c_q, c_kv, c_gn, c_gm = 2 * a, 3 * a, 3 * a + 1536, 3 * a + 1536 + 48
    wb = w_in.astype(BF16)
    wu, wv, wq = wb[:, :a], wb[:, a:2 * a], wb[:, c_q:c_kv]
    wkc, wkw = wb[:, c_kv:c_kv + 1024], wb[:, c_kv + 1024:c_gn]
    wgn = jnp.pad(wb[:, c_gn:c_gm], ((0, 0), (0, LANES - 48)))
    wgm = wb[:, c_gm:]
    row = lambda n: pl.BlockSpec((tm, n), lambda i: (i, 0))
    outs = [(a, BF16), (a, F32), (a, BF16), (1024, F32), (512, F32), (LANES, F32), (2 * D_MODEL, BF16),
            (1024, BF16), (512, BF16)]
    return pl.pallas_call(
        _inproj_kernel,
        grid=(t // tm,),
        in_specs=[row(D_MODEL), _full((1, D_MODEL)), _full(wu.shape), _full(wv.shape), _full(wq.shape),
                  _full(wkc.shape), _full(wkw.shape), _full(wgn.shape), _full(wgm.shape), _full((1, a))],
        out_specs=[row(n) for n, _ in outs],
        out_shape=[jax.ShapeDtypeStruct((t, n), d) for n, d in outs],
        compiler_params=_params(("parallel",)),
        name="inproj",
    )(x, gain.reshape(1, -1), wu, wv, wq, wkc, wkw, wgn, wgm, v_gain.reshape(1, -1))


def _half_block_products(tap, wq_ref, slot, nrow):
    acc = jnp.zeros((nrow, 4 * CMP_HIDDEN), F32)
    for r2 in range(CMP_STRIDE // 2):
        lhs = jnp.concatenate([tap(2 * r2), tap(2 * r2 + 1)], axis=1).astype(BF16)
        acc = acc + _dot(lhs, wq_ref[slot, r2 * 2 * LANES:(r2 + 1) * 2 * LANES, :])
    return acc


def _block_summaries(acc, gi, pe_row, w2, nrow):
    c0 = gi * 2 * CMP_HIDDEN
    pre = acc[:, c0:c0 + CMP_HIDDEN] + pltpu.roll(acc[:, c0 + CMP_HIDDEN:c0 + 2 * CMP_HIDDEN], nrow - 1, 0) + pe_row
    return _dot(_gelu(pre).astype(BF16), w2)


def _cmp_prompt_kernel(kv_ref, pe_ref, w1_ref, wq_ref, w2_ref, o_ref, pt_ref):
    nrow = kv_ref.shape[0] // CMP_STRIDE
    pe_term = _dot(pe_ref[0], w1_ref[0])
    acc = _half_block_products(lambda r: kv_ref[pl.ds(r, nrow, stride=CMP_STRIDE), :], wq_ref, 0, nrow)
    out = jnp.zeros((nrow, LANES), F32)
    for gi in range(2):
        out = out + _block_summaries(acc, gi, pe_term[0:1], w2_ref[0, gi], nrow)
    o_ref[0, 0, 0] = out.astype(BF16)
    pt_ref[0] = pe_term


def _cmp_prompt(kvc, b, s, pe_flat, w1, wq, w2pair):
    nrow = s // CMP_STRIDE
    return pl.pallas_call(
        _cmp_prompt_kernel,
        grid=(2, b, 2),
        in_specs=[pl.BlockSpec((s, LANES), lambda sl, i, p: (i, 2 * sl + p)),
                  pl.BlockSpec((1,) + pe_flat.shape[1:], lambda sl, i, p: (sl, 0, 0)),
                  pl.BlockSpec((1,) + w1.shape[1:], lambda sl, i, p: (sl, 0, 0)),
                  pl.BlockSpec((1,) + wq.shape[1:], lambda sl, i, p: (sl, 0, 0)),
                  pl.BlockSpec((1,) + w2pair.shape[1:], lambda sl, i, p: (sl, 0, 0, 0))],
        out_specs=[pl.BlockSpec((1, 1, 1, nrow, LANES), lambda sl, i, p: (sl, i, p, 0, 0)),
                   pl.BlockSpec((1, 8, CMP_HIDDEN), lambda sl, i, p: (sl, 0, 0))],
        out_shape=[jax.ShapeDtypeStruct((2, b, 2, nrow, LANES), BF16),
                   jax.ShapeDtypeStruct((2, 8, CMP_HIDDEN), F32)],
        compiler_params=_params(("arbitrary", "arbitrary", "arbitrary")),
        name="cmp_prompt",
    )(kvc, pe_flat, w1, wq, w2pair)


def _top_blocks(score, n_sel, n_top):
    row = lax.broadcasted_iota(I32, score.shape, 0)
    cnt = jnp.zeros(score.shape, F32)
    for j in range(n_sel):
        sj = score[j:j + 1, :]
        beats = jnp.where(sj > score, 1.0, jnp.where(sj == score, jnp.where(row > j, 1.0, 0.0), 0.0))
        cnt = cnt + beats
    return jnp.where((cnt < n_top) & (row < n_sel), 1.0, 0.0)


def _nsa_prompt_kernel(q_ref, kc_ref, vc_ref, ks_ref, vs_ref, kw_ref, vw_ref, bc_ref, bz_ref, ovt_ref, sp_ref,
                       pc_ref, gn_ref, o_ref, sel_ref, m_ref, l_ref, acc_ref, gt_ref, ob_ref, *, n_cmp, n_sel, n_top):
    tq = ATT_TILE
    tk = ATT_TILE
    cols = HPG * tq
    i = pl.program_id(2)
    q = jnp.concatenate([_dot(q_ref[...], sp_ref[0, h]).astype(BF16) for h in range(HPG)], axis=0)
    gt_ref[...] = gn_ref[...].T
    head0 = pl.program_id(1) * HPG

    def gate_row(branch):
        return jnp.concatenate([gt_ref[pl.ds(branch * N_HEADS + head0 + h, 1), :] for h in range(HPG)], axis=1)

    def to_tokens(o_t):
        out = jnp.zeros((tq, HPG * HEAD_DIM), F32)
        for h in range(HPG):
            out = out + _dot_tn(o_t[:, h * tq:(h + 1) * tq].astype(BF16), pc_ref[0, h])
        return out.astype(BF16)

    s = _dot_nt(kc_ref[0, 0, 0], q) + bc_ref[0, 0]
    c_io = lax.broadcasted_iota(I32, (LANES, cols), 0)
    l_io = lax.broadcasted_iota(I32, (LANES, cols), 1)
    qpos = i * tq + (l_io & (tq - 1))
    valid = (qpos >= c_io * CMP_STRIDE + (CMP_LEN - 1)) & (c_io < n_cmp)
    s = jnp.where(valid, s, NEG)
    e = jnp.exp2(s - jnp.max(s, axis=0, keepdims=True))
    p = jnp.where(valid, e / jnp.sum(e, axis=0, keepdims=True), 0.0)
    ob_ref[...] = gate_row(0) * _dot_tn(vc_ref[0, 0, 0], p.astype(BF16))

    psum = p[:, 0:tq] + p[:, tq:2 * tq] + p[:, 2 * tq:3 * tq] + p[:, 3 * tq:4 * tq]
    p_hi, p_lo = _hilo(psum)
    n_rows = -(-n_sel // 8) * 8
    imp = (_dot(ovt_ref[...], p_hi) + _dot(ovt_ref[...], p_lo))[0:n_rows]
    j_io = lax.broadcasted_iota(I32, (n_rows, tq), 0)
    qp2 = i * tq + lax.broadcasted_iota(I32, (n_rows, tq), 1)
    qblk = lax.shift_right_logical(qp2, int(math.log2(SEL_LEN)))
    forced = (j_io == 0) | (j_io == qblk) | (j_io == qblk - 1)
    score = jnp.where(forced, FORCE, jnp.where(j_io <= qblk, imp, NEG))
    score = jnp.where(j_io < n_sel, score, NEG_PAD)
    sel_ref[0:n_rows, :] = jnp.where(_top_blocks(score, n_sel, n_top) > 0.5, 0.0, NEG)

    def band_scores(k_ref, k0, nk, off, branch):
        k = k_ref[pl.ds(pl.multiple_of(k0, tq), nk), :]
        u0 = pl.multiple_of(jnp.maximum(BIAS_ZERO_ROW - off, 0), tq)
        return _dot_nt(k, q) + bz_ref[0, branch, pl.ds(u0, nk), :]

    k0 = jnp.maximum(i * tq - WINDOW, 0)
    s = band_scores(kw_ref, k0, WINDOW + tq, i * tq - k0, 1)
    pp = jnp.exp2(s - jnp.max(s, axis=0, keepdims=True))
    vw = vw_ref[pl.ds(pl.multiple_of(k0, tq), WINDOW + tq), :]
    ob_ref[...] += (gate_row(2) / jnp.sum(pp, axis=0, keepdims=True)) * _dot_tn(vw, pp.astype(BF16))

    tkc = 4 * tk
    per_chunk = tkc // SEL_LEN

    def slc_chunk(c):
        k0 = c * tkc
        rows = [jnp.broadcast_to(sel_ref[pl.ds(c * per_chunk + t, 1), :], (SEL_LEN, tq)) for t in range(per_chunk)]
        s = band_scores(ks_ref, k0, tkc, i * tq - k0, 0) + jnp.concatenate([jnp.concatenate(rows, axis=0)] * HPG, axis=1)
        return s, vs_ref[pl.ds(pl.multiple_of(k0, tq), tkc), :]

    c_diag = lax.shift_right_logical(i, 2)
    s, v = slc_chunk(c_diag)
    m = jnp.max(s, axis=0, keepdims=True)
    pp = jnp.exp2(s - m)
    m_ref[...] = jnp.broadcast_to(m, m_ref.shape)
    l_ref[...] = jnp.broadcast_to(jnp.sum(pp, axis=0, keepdims=True), l_ref.shape)
    acc_ref[...] = _dot_tn(v, pp.astype(BF16))

    def earlier_chunk(c, carry):
        s, v = slc_chunk(c)
        m_prev = m_ref[0:1, :]
        m_new = jnp.maximum(m_prev, jnp.max(s, axis=0, keepdims=True))
        alpha = jnp.exp2(m_prev - m_new)
        pp = jnp.exp2(s - m_new)
        l_ref[...] = jnp.broadcast_to(alpha * l_ref[0:1, :] + jnp.sum(pp, axis=0, keepdims=True), l_ref.shape)
        acc_ref[...] = alpha * acc_ref[...] + _dot_tn(v, pp.astype(BF16))
        m_ref[...] = jnp.broadcast_to(m_new, m_ref.shape)
        return carry

    lax.fori_loop(0, c_diag, earlier_chunk, 0)
    o_ref[...] = to_tokens(ob_ref[...] + (gate_row(1) / l_ref[0:1, :]) * acc_ref[...])


def _nsa_prompt(q, kcv, kvc, kvw, gn, b, bias_cmp, btile, ovt):
    s = kcv.shape[3] * CMP_STRIDE
    tq = ATT_TILE
    nq = s // tq
    n_cmp = (s - CMP_LEN) // CMP_STRIDE + 1
    n_sel = -(-s // SEL_LEN)
    n_top = min(SEL_TOP, n_sel)
    assert kcv.shape[3] == LANES and s % tq == 0 and WINDOW % tq == 0
    spread = np.zeros((2, HPG, HPG * HEAD_DIM, LANES), np.float32)
    dd = np.arange(HEAD_DIM)
    for gi in range(2):
        for h in range(HPG):
            spread[gi, h, h * HEAD_DIM + dd, gi * HEAD_DIM + dd] = 1.0
    collect = jnp.asarray(spread.transpose(0, 1, 3, 2), BF16)
    spread = jnp.asarray(spread, BF16)
    cols = HPG * tq
    kv_lane = lambda blk: pl.BlockSpec((s, LANES), lambda bi, gi, i: (bi, blk + gi // 2))
    cspec = lambda slot: pl.BlockSpec((1, 1, 1, LANES, LANES), lambda bi, gi, i: (slot, bi, gi // 2, 0, 0))
    return pl.pallas_call(
        functools.partial(_nsa_prompt_kernel, n_cmp=n_cmp, n_sel=n_sel, n_top=n_top),
        grid=(b, N_KV, nq),
        in_specs=[pl.BlockSpec((tq, HPG * HEAD_DIM), lambda bi, gi, i: (bi * nq + i, gi)),
                  cspec(0), cspec(1), kv_lane(4), kv_lane(6), kv_lane(0), kv_lane(2),
                  pl.BlockSpec((1, 1, LANES, cols), lambda bi, gi, i: (gi, i, 0, 0)),
                  pl.BlockSpec((1,) + btile.shape[1:], lambda bi, gi, i: (gi, 0, 0, 0)),
                  _full(ovt.shape),
                  pl.BlockSpec((1,) + spread.shape[1:], lambda bi, gi, i: (gi % 2, 0, 0, 0)),
                  pl.BlockSpec((1,) + collect.shape[1:], lambda bi, gi, i: (gi % 2, 0, 0, 0)),
                  pl.BlockSpec((tq, LANES), lambda bi, gi, i: (bi * nq + i, 0))],
        out_specs=pl.BlockSpec((tq, HPG * HEAD_DIM), lambda bi, gi, i: (bi * nq + i, gi)),
        out_shape=jax.ShapeDtypeStruct((b * s, N_HEADS * HEAD_DIM), BF16),
        scratch_shapes=[pltpu.VMEM((LANES, tq), F32), pltpu.VMEM((8, cols), F32), pltpu.VMEM((8, cols), F32),
                        pltpu.VMEM((LANES, cols), F32), pltpu.VMEM((LANES, tq), F32), pltpu.VMEM((LANES, cols), F32)],
        compiler_params=_params(("parallel", "parallel", "arbitrary")),
        name="nsa_prompt",
    )(q, kcv, kcv, kvc, kvc, kvw, kvw, bias_cmp, btile, ovt, spread, collect, gn)


def _nsa_sample_kernel(pt_ref, *refs, n_pages, n_sel, n_top):
    pages = refs[:n_pages]
    (q_ref, kvcn_ref, kvwn_ref, swin_ref, wq_ref, pet_ref, w2p_ref, bct_ref, bst_ref, bwt_ref, mwt_ref, mnt_ref,
     ovt_ref, hsum_ref, hm_ref, fold_ref, foldt_ref, selq_ref, gn_ref, gmask_ref, _, o_ref, win_ref, s_ref,
     rows_ref) = refs[n_pages:]
    del pt_ref
    gn_rep = jnp.concatenate([gn_ref[0]] * N_HEADS, axis=0)

    def gate_col(branch):
        return jnp.sum(gn_rep * gmask_ref[branch], axis=1, keepdims=True)
    gd = N_KV * HEAD_DIM
    nrow = n_pages * (LANES // CMP_STRIDE)
    for k, pg in enumerate(pages):
        for c in range(4):
            rows_ref[c, k * LANES:(k + 1) * LANES, :] = pg[0, 0, c * LANES:(c + 1) * LANES, :].T
    q_rep = jnp.concatenate([q_ref[0].astype(F32)] * N_HEADS, axis=0) * hm_ref[...]
    qr32 = _dot(q_rep.astype(BF16), fold_ref[...])
    qr = qr32.astype(BF16)
    qr_t = qr32.T.astype(BF16)

    def to_tokens(o):
        x = _dot(o.astype(BF16), foldt_ref[...]) * hm_ref[...]
        return _dot(selq_ref[...], x.astype(BF16)).astype(BF16)

    def page_slot_t(k, slot):
        return pages[k][0, 0, slot * gd:(slot + 1) * gd, :].astype(BF16)

    kc_all = []
    for slot in range(2):
        out = jnp.zeros((nrow, gd), F32)
        for pair in range(2):
            chunk = 2 * slot + pair

            def tap(r, chunk=chunk):
                return rows_ref[chunk, pl.ds(r, nrow, stride=CMP_STRIDE), :]

            acc = _half_block_products(tap, wq_ref, slot, nrow)
            for gi in range(2):
                out = out + _block_summaries(acc, gi, pet_ref[slot, 0:1], w2p_ref[slot, 2 * pair + gi], nrow)
        kc_all.append(out.astype(BF16))

    cols = LANES

    def softmax_t(s):
        m = jnp.max(s, axis=0, keepdims=True)
        e = jnp.exp2(s - m)
        return e / jnp.sum(e, axis=0, keepdims=True)

    def as_column(row):
        return jnp.broadcast_to(row, (cols, cols)).T[:, 0:1]

    n_cmp = nrow - 1
    c_io = lax.broadcasted_iota(I32, (nrow, cols), 0)
    s = _dot_nt(kc_all[0], qr) + bct_ref[...]
    valid = c_io < n_cmp
    p = jnp.where(valid, softmax_t(jnp.where(valid, s, NEG)), 0.0)
    o_b = gate_col(0) * _dot_tn(p.astype(BF16), kc_all[1])

    pg_sum = _dot_hilo_l(p, hsum_ref[...])
    g_hi, g_lo = _hilo(pg_sum)
    imp = _dot(ovt_ref[...], g_hi) + _dot(ovt_ref[...], g_lo)
    j_io = lax.broadcasted_iota(I32, imp.shape, 0)
    qblk = n_sel - 1
    forced = (j_io == 0) | (j_io == qblk) | (j_io == qblk - 1)
    score = jnp.where(forced, FORCE, jnp.where(j_io <= qblk, imp, NEG))
    score = jnp.where(j_io < n_sel, score, NEG_PAD)
    sel = _top_blocks(score, n_sel, n_top)

    per_page = LANES // SEL_LEN
    for k in range(n_pages):
        sk = _dot_tn(page_slot_t(k, 2), qr_t) + bst_ref[k * LANES:(k + 1) * LANES, :]
        mk = jnp.concatenate([jnp.broadcast_to(sel[per_page * k + t:per_page * k + t + 1, :], (SEL_LEN, cols))
                              for t in range(per_page)], axis=0)
        s_ref[k * LANES:(k + 1) * LANES, :] = jnp.where(mk > 0.5, sk, NEG)
    zpad = jnp.zeros((LANES - 8, gd), F32)
    kn = jnp.concatenate([kvcn_ref[0][:, 2 * gd:3 * gd], zpad], axis=0).astype(BF16)
    vn = jnp.concatenate([kvcn_ref[0][:, 3 * gd:4 * gd], zpad], axis=0).astype(BF16)
    sn = _dot_nt(kn, qr) + bst_ref[n_pages * LANES:(n_pages + 1) * LANES, :]
    mn = (mnt_ref[...] > 0.5) & (jnp.broadcast_to(sel[n_sel - 1:n_sel, :], (LANES, cols)) > 0.5)
    s_ref[n_pages * LANES:(n_pages + 1) * LANES, :] = jnp.where(mn, sn, NEG)
    n_keys = (n_pages + 1) * LANES
    m = jnp.max(s_ref[0:n_keys, :], axis=0, keepdims=True)
    den = jnp.zeros((1, cols), F32)
    o_t = jnp.zeros((gd, cols), F32)
    for k in range(n_pages):
        pk = jnp.exp2(s_ref[k * LANES:(k + 1) * LANES, :] - m)
        den = den + jnp.sum(pk, axis=0, keepdims=True)
        o_t = o_t + _dot(page_slot_t(k, 3), pk.astype(BF16))
    pk = jnp.exp2(s_ref[n_pages * LANES:(n_pages + 1) * LANES, :] - m)
    den = den + jnp.sum(pk, axis=0, keepdims=True)
    o = o_t.T + _dot_tn(pk.astype(BF16), vn)
    o_b = o_b + (gate_col(1) / as_column(den)) * o

    w_buf = swin_ref.shape[3]
    kwn = jnp.concatenate([kvwn_ref[0][:, 0:gd], zpad], axis=0).astype(BF16)
    vwn = jnp.concatenate([kvwn_ref[0][:, gd:2 * gd], zpad], axis=0).astype(BF16)
    sw = jnp.concatenate([_dot_tn(swin_ref[0, 0, 0:gd, :].astype(BF16), qr_t), _dot_nt(kwn, qr)], axis=0) + bwt_ref[...]
    sw = jnp.where(mwt_ref[...] > 0.5, sw, NEG)
    mw = jnp.max(sw, axis=0, keepdims=True)
    pw = jnp.exp2(sw - mw)
    denw = jnp.sum(pw, axis=0, keepdims=True)
    ow = (_dot(swin_ref[0, 0, gd:2 * gd, :].astype(BF16), pw[0:w_buf].astype(BF16)).T
          + _dot_tn(pw[w_buf:].astype(BF16), vwn))
    o_ref[0] = to_tokens(o_b + (gate_col(2) / as_column(denw)) * ow)

    n_new = kvwn_ref.shape[1]
    shifted = pltpu.roll(swin_ref[0, 0], w_buf - n_new, 1)
    new_t = jnp.concatenate([kvwn_ref[0], jnp.zeros((LANES - n_new, 2 * gd), F32)], axis=0).T
    tail = pltpu.roll(new_t, LANES - n_new, 1)
    lane = lax.broadcasted_iota(I32, (2 * gd, LANES), 1)
    win_ref[0, 0, :, 0:w_buf - LANES] = shifted[:, 0:w_buf - LANES]
    win_ref[0, 0, :, w_buf - LANES:w_buf] = jnp.where(lane >= LANES - n_new, tail, shifted[:, w_buf - LANES:w_buf])


def _nsa_sample(layer, win_prev, page_flat, cache4, q_s, gn_s, kvcn, kvwn, swin, wq, pet, w2p,
                bct, bst, bwt, mwt, mnt, ovt, hsum, n_sel):
    bs, n_new, d = q_s.shape
    r_head = np.arange(LANES) // n_new
    c_head = np.arange(d) // HEAD_DIM
    hm = jnp.asarray((r_head[:, None] == c_head[None]).astype(np.float32))
    fold_np = np.zeros((d, N_KV * HEAD_DIM), np.float32)
    fold_np[np.arange(d), (c_head // HPG) * HEAD_DIM + np.arange(d) % HEAD_DIM] = 1.0
    fold, foldt = jnp.asarray(fold_np, BF16), jnp.asarray(fold_np.T, BF16)
    selq = jnp.asarray((np.arange(n_new)[:, None] == (np.arange(LANES) % n_new)[None]).astype(np.float32), BF16)
    gmask = jnp.asarray((np.arange(3)[:, None, None] * N_HEADS + r_head[None, :, None]
                         == np.arange(LANES)[None, None, :]).astype(np.float32))
    n_pages = page_flat.shape[0] // bs
    w_buf = swin.shape[3]
    n_top = min(SEL_TOP, n_sel)
    page_specs = [pl.BlockSpec((1, 1, cache4.shape[2], LANES),
                               functools.partial(lambda b, pt, k: (layer, pt[b * n_pages + k], 0, 0), k=k))
                  for k in range(n_pages)]
    cfull = lambda a: pl.BlockSpec(a.shape, lambda b, pt: (0,) * a.ndim)
    in_specs = page_specs + [
        pl.BlockSpec((1, n_new, d), lambda b, pt: (b, 0, 0)),
        pl.BlockSpec((1,) + kvcn.shape[1:], lambda b, pt: (b, 0, 0)),
        pl.BlockSpec((1,) + kvwn.shape[1:], lambda b, pt: (b, 0, 0)),
        pl.BlockSpec((1, 1, swin.shape[2], w_buf), lambda b, pt: (layer, b, 0, 0)),
        cfull(wq), cfull(pet), cfull(w2p), cfull(bct), cfull(bst), cfull(bwt), cfull(mwt), cfull(mnt),
        cfull(ovt), cfull(hsum), cfull(hm), cfull(fold), cfull(foldt), cfull(selq),
        pl.BlockSpec((1, n_new, LANES), lambda b, pt: (b, 0, 0)), cfull(gmask)]
    in_specs.append(pl.BlockSpec(memory_space=pl.ANY))
    args = [cache4] * n_pages + [q_s, kvcn, kvwn, swin, wq, pet, w2p, bct, bst, bwt, mwt, mnt, ovt, hsum,
                                 hm, fold, foldt, selq, gn_s, gmask, win_prev]
    grid_spec = pltpu.PrefetchScalarGridSpec(
        num_scalar_prefetch=1, grid=(bs,), in_specs=in_specs,
        out_specs=[pl.BlockSpec((1, n_new, d), lambda b, pt: (b, 0, 0)),
                   pl.BlockSpec((1, 1, swin.shape[2], w_buf), lambda b, pt: (layer, b, 0, 0))],
        scratch_shapes=[pltpu.VMEM(((n_pages + 1) * LANES, LANES), F32),
                        pltpu.VMEM((4, n_pages * LANES, LANES), F32)])
    return pl.pallas_call(
        functools.partial(_nsa_sample_kernel, n_pages=n_pages, n_sel=n_sel, n_top=n_top), grid_spec=grid_spec,
        out_shape=[jax.ShapeDtypeStruct((bs, n_new, d), BF16), jax.ShapeDtypeStruct(swin.shape, F32)],
        input_output_aliases={len(args): 1},
        compiler_params=_params(("arbitrary",)),
        name="nsa_sample",
    )(page_flat, *args)


def _merge_kernel(x_ref, gu_ref, v_ref, obp_ref, obs_ref, gm_ref, wmix_ref, bmix_ref, wb_ref,
                  wo_ref, nf_ref, wrh_ref, wrl_ref, br_ref, x1_o, h2_o, ei_o, rw_o, *, n_prompt_tiles):
    tm = x_ref.shape[0]
    is_prompt = pl.program_id(0) < n_prompt_tiles
    mixed = []
    for c in range(tm // CHUNK):
        vb = v_ref[c * CHUNK:(c + 1) * CHUNK, :].astype(BF16)
        mixed.append(jnp.concatenate(
            [_dot(wmix_ref[0, g], vb[:, g * A_GROUP_WIDTH:(g + 1) * A_GROUP_WIDTH]) for g in range(A_GROUPS)],
            axis=1) + bmix_ref[0])
    o_a = gu_ref[...].astype(F32) * jnp.concatenate(mixed, axis=0)
    o_b = jnp.where(is_prompt, obp_ref[...], obs_ref[...])
    gm = gm_ref[...].astype(F32)
    merged = (gm[:, :D_MODEL] * _dot(o_a.astype(BF16), wb_ref[0])
              + gm[:, D_MODEL:] * _dot(o_b, wb_ref[1]))
    x1 = x_ref[...] + _dot(merged.astype(BF16), wo_ref[...])
    x1_o[...] = x1
    h2 = _rms(x1, nf_ref[...])
    h2_o[...] = h2
    hh, hl = _hilo(h2)
    logit = _dot(hh, wrh_ref[...]) + _dot(hl, wrh_ref[...]) + _dot(hh, wrl_ref[...]) + br_ref[...]
    lane = lax.broadcasted_iota(I32, logit.shape, 1)
    big = jnp.int32(9999)
    is_g = lane < N_GROUPS
    gl = jnp.where(is_g, logit, -jnp.inf)
    gmax = jnp.max(gl, axis=-1, keepdims=True)
    grp = jnp.min(jnp.where(gl == gmax, lane, big), axis=-1, keepdims=True)
    p_grp = 1.0 / jnp.sum(jnp.where(is_g, jnp.exp(logit - gmax), 0.0), axis=-1, keepdims=True)
    e_lane = lane - N_GROUPS
    in_grp = (e_lane >= 0) & (lax.shift_right_arithmetic(e_lane, 3) == grp) & (e_lane < N_EXPERTS)
    el = jnp.where(in_grp, logit, -jnp.inf)
    t1 = jnp.max(el, axis=-1, keepdims=True)
    i1 = jnp.min(jnp.where(el == t1, lane, big), axis=-1, keepdims=True)
    el2 = jnp.where(lane == i1, -jnp.inf, el)
    t2 = jnp.max(el2, axis=-1, keepdims=True)
    i2 = jnp.min(jnp.where(el2 == t2, lane, big), axis=-1, keepdims=True)
    r = jnp.exp(t2 - t1)
    w1 = p_grp / (1.0 + r)
    w2 = p_grp * r / (1.0 + r)
    ei_o[...] = jnp.where(lane == 0, i1 - N_GROUPS, jnp.where(lane == 1, i2 - N_GROUPS, 0))
    rw_o[...] = jnp.where(lane == 0, w1, jnp.where(lane == 1, w2, 0.0))


def _merge(x, gu, v, obp, obs, gm, wmix, bmix, wb, wo, nf, wrh, wrl, brr, n_prompt_tiles):
    t = x.shape[0]
    tm = TOK_TILE
    row = lambda n: pl.BlockSpec((tm, n), lambda i: (i, 0))
    kind = lambda i: jnp.where(i < n_prompt_tiles, 0, 1)
    outs = [(D_MODEL, F32), (D_MODEL, F32), (LANES, I32), (LANES, F32)]
    return pl.pallas_call(
        functools.partial(_merge_kernel, n_prompt_tiles=n_prompt_tiles),
        grid=(t // tm,),
        in_specs=[row(D_MODEL), row(A_WIDTH), row(A_WIDTH),
                  pl.BlockSpec((tm, D_MODEL), lambda i: (jnp.minimum(i, n_prompt_tiles - 1), 0)),
                  pl.BlockSpec((tm, D_MODEL), lambda i: (jnp.maximum(i - n_prompt_tiles, 0), 0)),
                  row(2 * D_MODEL),
                  pl.BlockSpec((1, A_GROUPS, CHUNK, CHUNK), lambda i: (kind(i), 0, 0, 0)),
                  pl.BlockSpec((1, CHUNK, A_WIDTH), lambda i: (kind(i), 0, 0)),
                  _full(wb.shape), _full(wo.shape), _full((1, D_MODEL)),
                  _full(wrh.shape), _full(wrl.shape), _full((1, LANES))],
        out_specs=[row(n) for n, _ in outs],
        out_shape=[jax.ShapeDtypeStruct((t, n), d) for n, d in outs],
        compiler_params=_params(("parallel",)),
        name="merge",
    )(x, gu, v, obp, obs, gm, wmix, bmix, wb, wo, nf.reshape(1, -1), wrh, wrl, brr)


def _route1_kernel(ei_ref, ltri_ref, rank_o, cnt_o, carry):
    @pl.when(pl.program_id(0) == 0)
    def _():
        carry[...] = jnp.zeros(carry.shape, F32)

    ei = ei_ref[...]
    lane = lax.broadcasted_iota(I32, ei.shape, 1)
    e1, e2 = ei[:, 0:1], ei[:, 1:2]
    oh = jnp.where((lane == e1) | (lane == e2), 1.0, 0.0)
    cum = _dot(ltri_ref[...], oh.astype(BF16)) + carry[0:1, :]
    r1 = jnp.sum(jnp.where(lane == e1, cum, 0.0), axis=-1, keepdims=True)
    r2 = jnp.sum(jnp.where(lane == e2, cum, 0.0), axis=-1, keepdims=True)
    rank_o[...] = jnp.where(lane == 0, r1, jnp.where(lane == 1, r2, 0.0)).astype(I32)
    carry[...] = carry[...] + jnp.sum(oh, axis=0, keepdims=True)
    cnt_o[...] = carry[...]


def _route2_kernel(ei_ref, rank_ref, cnt_ref, utri_ref, dest_o, be_o, nu_o):
    shift = int(math.log2(MOE_BLOCK))
    nb = lax.shift_right_logical(cnt_ref[...].astype(I32) + (MOE_BLOCK - 1), shift).astype(F32)
    start = _dot(nb.astype(BF16), utri_ref[...])
    ei = ei_ref[...]
    lane = lax.broadcasted_iota(I32, ei.shape, 1)
    e1, e2 = ei[:, 0:1], ei[:, 1:2]
    s1 = jnp.sum(jnp.where(lane == e1, start[0:1, :], 0.0), axis=-1, keepdims=True)
    s2 = jnp.sum(jnp.where(lane == e2, start[0:1, :], 0.0), axis=-1, keepdims=True)
    rk = rank_ref[...]
    d1 = s1.astype(I32) * MOE_BLOCK + rk[:, 0:1]
    d2 = s2.astype(I32) * MOE_BLOCK + rk[:, 1:2]
    dest_o[...] = jnp.where(lane == 0, d1, jnp.where(lane == 1, d2, 0))
    end = start[0:1, :] + nb[0:1, :]
    j = lax.broadcasted_iota(I32, be_o.shape, 0).astype(F32)
    l2 = lax.broadcasted_iota(I32, be_o.shape, 1)
    ge = jnp.sum(jnp.where((l2 < N_EXPERTS) & (end <= j), 1.0, 0.0), axis=-1, keepdims=True)
    be_o[...] = jnp.broadcast_to(jnp.minimum(ge, N_EXPERTS - 1.0).astype(I32), be_o.shape)
    l3 = lax.broadcasted_iota(I32, nu_o.shape, 1)
    nu = jnp.sum(jnp.where(l3 == N_EXPERTS - 1, jnp.broadcast_to(end, nu_o.shape), 0.0), axis=-1, keepdims=True)
    nu_o[...] = jnp.broadcast_to(nu.astype(I32), nu_o.shape)


def _route(ei, n_blocks):
    t = ei.shape[0]
    tm = TOK_TILE
    row = pl.BlockSpec((tm, LANES), lambda i: (i, 0))
    ltri = jnp.asarray(np.tril(np.ones((tm, tm), np.float32), -1), BF16)
    utri = jnp.asarray(np.triu(np.ones((LANES, LANES), np.float32), 1), BF16)
    rank, cnt = pl.pallas_call(
        _route1_kernel, grid=(t // tm,),
        in_specs=[row, _full((tm, tm))],
        out_specs=[row, _full((8, LANES))],
        out_shape=[jax.ShapeDtypeStruct((t, LANES), I32), jax.ShapeDtypeStruct((8, LANES), F32)],
        scratch_shapes=[pltpu.VMEM((8, LANES), F32)],
        compiler_params=_params(("arbitrary",)),
        name="route_rank",
    )(ei, ltri)
    nbp = -(-n_blocks // 8) * 8
    dest, be, nu = pl.pallas_call(
        _route2_kernel, grid=(t // tm,),
        in_specs=[row, row, _full((8, LANES)), _full((LANES, LANES))],
        out_specs=[row, _full((nbp, LANES)), _full((8, LANES))],
        out_shape=[jax.ShapeDtypeStruct((t, LANES), I32), jax.ShapeDtypeStruct((nbp, LANES), I32),
                   jax.ShapeDtypeStruct((8, LANES), I32)],
        compiler_params=_params(("arbitrary",)),
        name="route_slots",
    )(ei, rank, cnt, utri)
    return dest[:, :2].reshape(-1), be[:n_blocks, 0], nu[0, :1]


def _dispatch_kernel(dest_ref, h_ref, xs_in, xs_out, sem, *, chunk):
    del xs_in
    i = pl.program_id(0)

    def row_copy(r, d):
        return pltpu.make_async_copy(h_ref.at[pl.ds(r, 1)], xs_out.at[pl.ds(d, 1)], sem)

    def issue(r, c):
        t = i * chunk + r
        row_copy(r, dest_ref[2 * t]).start()
        row_copy(r, dest_ref[2 * t + 1]).start()
        return c

    lax.fori_loop(0, chunk, issue, 0, unroll=8)
    for _ in range(2):
        pltpu.make_async_copy(h_ref, xs_out.at[pl.ds(0, chunk)], sem).wait()


def _dispatch(dest_flat, h2, cap):
    t, d = h2.shape
    chunk = TOK_TILE
    grid_spec = pltpu.PrefetchScalarGridSpec(
        num_scalar_prefetch=1, grid=(t // chunk,),
        in_specs=[pl.BlockSpec((chunk, d), lambda i, dr: (i, 0)), pl.BlockSpec(memory_space=pl.ANY)],
        out_specs=pl.BlockSpec(memory_space=pl.ANY),
        scratch_shapes=[pltpu.SemaphoreType.DMA(())])
    return pl.pallas_call(
        functools.partial(_dispatch_kernel, chunk=chunk), grid_spec=grid_spec,
        out_shape=jax.ShapeDtypeStruct((cap, d), h2.dtype),
        input_output_aliases={2: 0},
        compiler_params=_params(("arbitrary",)),
        name="moe_dispatch",
    )(dest_flat, h2, jnp.zeros((cap, d), h2.dtype))


def _expert_kernel(be_ref, nu_ref, x_ref, wg_ref, wu_ref, wd_ref, y_ref, wgb, wub, wdb):
    i = pl.program_id(0)

    @pl.when((i == 0) | (be_ref[i] != be_ref[jnp.maximum(i - 1, 0)]))
    def _():
        wgb[...] = wg_ref[0, 0].astype(BF16)
        wub[...] = wu_ref[0, 0].astype(BF16)
        wdb[...] = wd_ref[0, 0].astype(BF16)

    @pl.when(i < nu_ref[0])
    def _():
        xb = x_ref[...].astype(BF16)
        g = _dot(xb, wgb[...])
        u = _dot(xb, wub[...])
        a = (g * _sigmoid(g) * u).astype(BF16)
        y_ref[...] = _dot(a, wdb[...])

    @pl.when(i >= nu_ref[0])
    def _():
        y_ref[...] = jnp.zeros(y_ref.shape, F32)


def _experts(layer, be, nu, xs, wg, wu, wd):
    cap, d = xs.shape
    bm = MOE_BLOCK
    xin = lambda i, be, nu: (jnp.minimum(i, jnp.maximum(nu[0] - 1, 0)), 0)
    wsel = lambda i, be, nu: (layer, be[i], 0, 0)
    grid_spec = pltpu.PrefetchScalarGridSpec(
        num_scalar_prefetch=2, grid=(cap // bm,),
        in_specs=[pl.BlockSpec((bm, d), xin),
                  pl.BlockSpec((1, 1, d, D_EXPERT), wsel), pl.BlockSpec((1, 1, d, D_EXPERT), wsel),
                  pl.BlockSpec((1, 1, D_EXPERT, d), wsel)],
        out_specs=pl.BlockSpec((bm, d), lambda i, be, nu: (i, 0)),
        scratch_shapes=[pltpu.VMEM((d, D_EXPERT), BF16), pltpu.VMEM((d, D_EXPERT), BF16),
                        pltpu.VMEM((D_EXPERT, d), BF16)])
    return pl.pallas_call(
        _expert_kernel, grid_spec=grid_spec,
        out_shape=jax.ShapeDtypeStruct((cap, d), F32),
        compiler_params=_params(("arbitrary",)),
        name="moe_experts",
    )(be, nu, xs, wg, wu, wd)


def _combine_kernel(dest_ref, x_ref, rw_ref, pe_ref, y_hbm, np_ref, wg_ref, wp_ref, fn_ref, o_ref, ybuf, sem,
                    *, final):
    tm = x_ref.shape[0]
    i = pl.program_id(0)
    n = pl.num_programs(0)

    def row_copy(slot, k, r, d):
        return pltpu.make_async_copy(y_hbm.at[pl.ds(d, 1)], ybuf.at[slot, k, pl.ds(r, 1)], sem.at[slot])

    def issue(tile, slot):
        def body(r, c):
            t = tile * tm + r
            row_copy(slot, 0, r, dest_ref[2 * t]).start()
            row_copy(slot, 1, r, dest_ref[2 * t + 1]).start()
            return c
        lax.fori_loop(0, tm, body, 0, unroll=8)

    slot = i & 1

    @pl.when(i == 0)
    def _():
        issue(0, 0)

    @pl.when(i + 1 < n)
    def _():
        issue(i + 1, 1 - slot)

    for k in range(2):
        pltpu.make_async_copy(y_hbm.at[pl.ds(0, tm)], ybuf.at[slot, k], sem.at[slot]).wait()

    rw = rw_ref[...]
    x2 = x_ref[...] + rw[:, 0:1] * ybuf[slot, 0] + rw[:, 1:2] * ybuf[slot, 1]
    gate = _sigmoid(_dot(_rms(x2, np_ref[...]).astype(BF16), wg_ref[...]))
    x3 = x2 + gate * _dot(pe_ref[...].astype(BF16), wp_ref[...])
    o_ref[...] = _rms(x3, fn_ref[...]) if final else x3


def _combine(dest_flat, x1, rw, pemb, y, norm_ple, wg, wp, final_norm, final):
    t, d = x1.shape
    tm = TOK_TILE
    row = lambda n: pl.BlockSpec((tm, n), lambda i, dr: (i, 0))
    cfull = lambda shape: pl.BlockSpec(shape, lambda i, dr: (0,) * len(shape))
    grid_spec = pltpu.PrefetchScalarGridSpec(
        num_scalar_prefetch=1, grid=(t // tm,),
        in_specs=[row(d), row(LANES), row(PLE_DIM), pl.BlockSpec(memory_space=pl.ANY),
                  cfull((1, d)), cfull(wg.shape), cfull(wp.shape), cfull((1, d))],
        out_specs=row(d),
        scratch_shapes=[pltpu.VMEM((2, 2, tm, d), F32), pltpu.SemaphoreType.DMA((2,))])
    return pl.pallas_call(
        functools.partial(_combine_kernel, final=final), grid_spec=grid_spec,
        out_shape=jax.ShapeDtypeStruct((t, d), F32),
        compiler_params=_params(("arbitrary",)),
        name="moe_combine_ple",
    )(dest_flat, x1, rw, pemb, y, norm_ple.reshape(1, -1), wg, wp, final_norm.reshape(1, -1))


def _per_head_lookup(rel_bias, bkt):
    onehot = (jnp.arange(N_BUCKETS, dtype=I32)[:, None] == jnp.asarray(bkt.reshape(1, -1), I32)).astype(F32)
    tab = jnp.dot(rel_bias.T, onehot, precision=lax.Precision.HIGHEST)
    return tab.reshape((N_HEADS,) + bkt.shape)


def _prompt_tables(rel_bias, s):
    tq = ATT_TILE
    nq = s // tq
    n_cmp = (s - CMP_LEN) // CMP_STRIDE + 1
    n_sel = -(-s // SEL_LEN)
    c = np.arange(LANES)
    qpos = np.arange(s).reshape(nq, 1, tq)
    bkt = _bucket_np(qpos - (c * CMP_STRIDE + CMP_LEN - 1)[None, :, None])
    bias_cmp = _per_head_lookup(rel_bias, bkt).reshape(N_KV, HPG, nq, LANES, tq)
    bias_cmp = bias_cmp.transpose(0, 2, 3, 1, 4).reshape(N_KV, nq, LANES, HPG * tq)
    u, ql = np.arange(2 * BIAS_ZERO_ROW)[:, None], np.arange(tq)[None]
    assert (_bucket_np(np.arange(BIAS_ZERO_ROW - 4 * tq + 1, 2 * s)) == N_BUCKETS - 1).all()
    dist = ql - u + BIAS_ZERO_ROW
    btile = _per_head_lookup(rel_bias, _bucket_np(dist)).reshape(N_KV, HPG, 2 * BIAS_ZERO_ROW, tq)
    keep = np.stack([dist >= 0, (dist >= 0) & (dist < WINDOW)])[None, :, None]
    btile = jnp.where(jnp.asarray(keep), btile[:, None], NEG)
    btile = btile.transpose(0, 1, 3, 2, 4).reshape(N_KV, 2, 2 * BIAS_ZERO_ROW, HPG * tq)
    cs, ss = np.arange(n_cmp) * CMP_STRIDE, np.arange(n_sel) * SEL_LEN
    overlap = ((cs[:, None] < ss[None] + SEL_LEN) & (cs[:, None] + CMP_LEN > ss[None])).astype(np.float32)
    ovt = np.zeros((LANES, LANES), np.float32)
    ovt[:n_sel, :n_cmp] = overlap.T
    return bias_cmp, btile, jnp.asarray(ovt, BF16)


def _per_column_lookup(rel_bias, bkt, n_new):
    rows = bkt.shape[0]
    onehot = (jnp.asarray(np.repeat(bkt, N_BUCKETS, axis=1), I32)
              == jnp.asarray(np.tile(np.arange(N_BUCKETS), n_new)[None], I32)).astype(F32)
    spread = (rel_bias[None, :, :, None] * jnp.eye(n_new, dtype=F32)[:, None, None, :])
    spread = spread.reshape(n_new * N_BUCKETS, N_HEADS * n_new)
    return jnp.dot(onehot, spread, precision=lax.Precision.HIGHEST).reshape(rows, N_HEADS * n_new)


def _sample_tables(rel_bias, past, n_new, w_buf):
    cols = np.arange(LANES)
    qpos = past + cols % n_new
    qnew = past + np.arange(n_new)
    n_pages = past // LANES
    n_cmp = (past + n_new - CMP_LEN) // CMP_STRIDE + 1
    n_sel = -(-(past + n_new) // SEL_LEN)
    assert n_cmp == past // CMP_STRIDE - 1 and HPG * n_new * N_KV == LANES
    c = np.arange(past // CMP_STRIDE)
    bct = _per_column_lookup(rel_bias, _bucket_np(qnew[None] - (c * CMP_STRIDE + CMP_LEN - 1)[:, None]), n_new)
    key = np.arange((n_pages + 1) * LANES)
    bst = _per_column_lookup(rel_bias, _bucket_np(qnew[None] - key[:, None]), n_new)
    kw = np.arange(w_buf + LANES)
    kpos = np.where(kw < w_buf, past - w_buf + kw, past + kw - w_buf)
    bwt = _per_column_lookup(rel_bias, _bucket_np(qnew[None] - kpos[:, None]), n_new)
    dw = qpos[None] - kpos[:, None]
    mwt = ((dw >= 0) & (dw < WINDOW) & (kpos[:, None] >= 0) & (kw[:, None] < w_buf + n_new)).astype(np.float32)
    kn = np.arange(LANES)
    mnt = ((kn[:, None] < n_new) & (kn[:, None] <= (cols % n_new)[None])).astype(np.float32)
    cs, ss = np.arange(n_cmp) * CMP_STRIDE, np.arange(n_sel) * SEL_LEN
    overlap = ((cs[:, None] < ss[None] + SEL_LEN) & (cs[:, None] + CMP_LEN > ss[None])).astype(np.float32)
    rows_sel = -(-n_sel // 8) * 8
    ovt = np.zeros((rows_sel, past // CMP_STRIDE), np.float32)
    ovt[:n_sel, :n_cmp] = overlap.T
    same = (cols[:, None] // (HPG * n_new) == cols[None] // (HPG * n_new)) & \
           (cols[:, None] % n_new == cols[None] % n_new)
    return (bct, bst, bwt, jnp.asarray(mwt), jnp.asarray(mnt), jnp.asarray(ovt, BF16),
            jnp.asarray(same.astype(np.float32), BF16), n_sel)


def kernel(x_prompt, x_sample, cache_kv, state_win_kv, page_table, p_prompt, p_sample, rel_bias, norm_mix, w_in, v_norm, w_spatial, b_spatial, pe_cmp, w_phi1, w_phi2, w_branch, w_out, norm_ffn, w_router_group, b_router_group, w_router_expert, b_router_expert, w_exp_gate, w_exp_up, w_exp_down, norm_ple, w_ple_gate, w_ple_proj, final_norm):
    b, s, d = x_prompt.shape
    bs, n_new, _ = x_sample.shape
    depth = w_in.shape[0]
    n_pool, page = cache_kv.shape[1], cache_kv.shape[2]
    past = page_table.shape[1] * page
    w_buf = state_win_kv.shape[2]
    tp, ts = b * s, bs * n_new
    t = tp + ts
    assert page == LANES and tp % TOK_TILE == 0 and ts % TOK_TILE == 0 and n_new == 8 and s >= CHUNK

    x = jnp.concatenate([x_prompt.reshape(tp, d), x_sample.reshape(ts, d)], axis=0)
    pemb = jnp.concatenate([p_prompt.reshape(depth, tp, PLE_DIM), p_sample.reshape(depth, ts, PLE_DIM)], axis=1)
    cache4 = cache_kv.transpose(0, 1, 3, 4, 5, 2).reshape(depth, n_pool, -1, page)
    swin = state_win_kv.transpose(0, 1, 3, 4, 5, 2).reshape(depth, bs, -1, w_buf)
    page_flat = page_table.reshape(-1).astype(I32)

    rel_bias2 = rel_bias * LOG2E
    bias_cmp, btile, ovt_p = _prompt_tables(rel_bias2, s)
    bct, bst, bwt, mwt, mnt, ovt_s, hsum, n_sel_s = _sample_tables(rel_bias2, past, n_new, w_buf)

    tril = np.tril(np.ones((CHUNK, CHUNK), np.float32))
    blockdiag = np.kron(np.eye(CHUNK // n_new, dtype=np.float32), np.tril(np.ones((n_new, n_new), np.float32)))
    eye_g = jnp.eye(N_KV, dtype=BF16)
    eye_2 = jnp.eye(2, dtype=BF16)

    n_blocks = -(-2 * t // MOE_BLOCK) + N_EXPERTS
    cap = n_blocks * MOE_BLOCK

    kv_p, kv_s, win_p, v_s = [], [], [], []
    win_s = jnp.zeros(swin.shape, F32)
    for i in range(depth):
        gu, v, q, kvc, kvw, gn, gm, kvc16, kvw16 = _inproj(x, norm_mix[i], w_in[i], v_norm[i])
        kv_p.append(kvc[:tp].reshape(b, s, 4, N_KV, HEAD_DIM))
        kv_s.append(kvc[tp:].reshape(bs, n_new, 4, N_KV, HEAD_DIM))
        win_p.append(kvw[:tp].reshape(b, s, 2, N_KV, HEAD_DIM)[:, s - min(WINDOW, s):])
        v_s.append(v[tp:].reshape(bs, n_new, A_WIDTH))

        w1b = w_phi1[i].astype(BF16)
        w2b = w_phi2[i].astype(BF16)
        pe_flat = jnp.broadcast_to(pe_cmp[i].reshape(2, 1, CMP_LEN * HEAD_DIM), (2, 8, CMP_LEN * HEAD_DIM))
        w1r = w1b.reshape(2, 2, CMP_STRIDE, HEAD_DIM, CMP_HIDDEN)
        wq = jnp.einsum('sprdn,ij->sridjpn', w1r, eye_2).reshape(2, CMP_STRIDE * LANES, 4 * CMP_HIDDEN)
        w2p = jnp.einsum('skd,gh->sgkhd', w2b, eye_g).reshape(2, N_KV, CMP_HIDDEN, N_KV * HEAD_DIM)
        w2pair = jnp.einsum('skd,gh->sgkhd', w2b, eye_2).reshape(2, 2, CMP_HIDDEN, LANES)

        kcv, pet = _cmp_prompt(kvc, b, s, pe_flat.astype(BF16), w1b, wq, w2pair)
        ob_p = _nsa_prompt(q, kcv, kvc16, kvw16, gn, b, bias_cmp, btile, ovt_p)

        ob_s, win_s = _nsa_sample(i, win_s, page_flat, cache4, q[tp:].reshape(bs, n_new, d),
                                  gn[tp:].reshape(bs, n_new, LANES),
                                  kvc[tp:].reshape(bs, n_new, -1), kvw[tp:].reshape(bs, n_new, -1), swin,
                                  wq, pet, w2p, bct, bst, bwt, mwt, mnt, ovt_s, hsum, n_sel_s)
        ob_s = ob_s.reshape(ts, d)

        ws = w_spatial[i]
        wmix = jnp.stack([ws * tril, jnp.tile(ws[:, :n_new, :n_new], (1, CHUNK // n_new, CHUNK // n_new)) * blockdiag])
        bsp = b_spatial[i]
        bmix = jnp.stack([jnp.repeat(bsp.T, A_GROUP_WIDTH, axis=1),
                          jnp.repeat(jnp.tile(bsp[:, :n_new], (1, CHUNK // n_new)).T, A_GROUP_WIDTH, axis=1)])
        wr = jnp.concatenate([w_router_group[i], w_router_expert[i]], axis=1)
        wr = jnp.pad(wr, ((0, 0), (0, LANES - wr.shape[1])))
        wrh = wr.astype(BF16)
        wrl = (wr - wrh.astype(F32)).astype(BF16)
        brr = jnp.pad(jnp.concatenate([b_router_group[i], b_router_expert[i]]), (0, LANES - N_GROUPS - N_EXPERTS))
        x1, h2, ei, rw = _merge(x, gu, v, ob_p, ob_s, gm, wmix.astype(BF16), bmix, w_branch[i].astype(BF16),
                                w_out[i].astype(BF16), norm_ffn[i], wrh, wrl, brr.reshape(1, LANES),
                                tp // TOK_TILE)

        dest_flat, be, nu = _route(ei, n_blocks)
        xs = _dispatch(dest_flat, h2, cap)
        y = _experts(i, be, nu, xs, w_exp_gate, w_exp_up, w_exp_down)
        x = _combine(dest_flat, x1, rw, pemb[i], y, norm_ple[i], w_ple_gate[i].astype(BF16),
                     w_ple_proj[i].astype(BF16), final_norm, final=(i == depth - 1))

    y_prompt = x[:tp].reshape(b, s, d)
    y_sample = x[tp:].reshape(bs, n_new, d)
    win_sample = win_s.reshape(depth, bs, 2, N_KV, HEAD_DIM, w_buf).transpose(0, 1, 5, 2, 3, 4)
    return (y_prompt, y_sample, jnp.stack(kv_p), jnp.stack(kv_s), jnp.stack(win_p), win_sample, jnp.stack(v_s))
```
